```python
import math
import jax
import jax.numpy as jnp
from jax import lax
import numpy as np

D_MODEL = 1024
BATCH = 32
SEQ = 2048
DEPTH = 4

CHUNK = 64
Q_BLOCK = 128
EPS = 1e-6

HEAD_DIM = 64
GROUP_HEADS = 4
GROUP_WIDTH = GROUP_HEADS * HEAD_DIM
N_MIXERS = 4
D_MIX = N_MIXERS * GROUP_WIDTH

MLA_HEADS = GROUP_HEADS
MLA_NOPE = 64
MLA_ROPE = 32
MLA_V = HEAD_DIM
MLA_Q_LORA = 192
MLA_KV_LORA = 128
ROPE_THETA = 10000.0

FOX_HEADS = GROUP_HEADS

CONV_CH = GROUP_WIDTH
CONV_WIDTH = 3

DIFF_HEADS = GROUP_HEADS
DIFF_HALF = HEAD_DIM // 2

T5_BUCKETS = 32
T5_MAX_DIST = 128

P_MLA = MLA_Q_LORA + MLA_KV_LORA + MLA_ROPE
P_FOX = 3 * GROUP_WIDTH + FOX_HEADS
P_CONV = 3 * CONV_CH
P_DIFF = 3 * GROUP_WIDTH
P_IN = P_MLA + P_FOX + P_CONV + P_DIFF

N_GROUPS = 8
EXPERTS_PER_GROUP = 8
N_EXPERTS = N_GROUPS * EXPERTS_PER_GROUP
TOP_K = 2
D_EXPERT = 512
MOE_BLOCK = 128
ADA_CHUNKS = 6

kernel_name = 'hybrid_chunk_causal_moe_trunk'


def rms_norm(t, g):
    tf = t.astype(jnp.float32)
    tf = tf * lax.rsqrt(jnp.mean(tf * tf, axis=-1, keepdims=True) + EPS)
    return (tf * g.astype(jnp.float32)).astype(t.dtype)


def to_heads(t, n_heads):
    return t.reshape(t.shape[0], t.shape[1], n_heads, -1).transpose(0, 2, 1, 3)


def from_heads(t):
    b, h, s, d = t.shape
    return t.transpose(0, 2, 1, 3).reshape(b, s, h * d)


def rope(t, positions):
    half = MLA_ROPE // 2
    inv_freq = ROPE_THETA ** (-jnp.arange(half, dtype=jnp.float32) / half)
    ang = positions.astype(jnp.float32)[:, :, None] * inv_freq
    ang = ang.reshape(ang.shape[:2] + (1,) * (t.ndim - 3) + (half,))
    cos, sin = jnp.cos(ang), jnp.sin(ang)
    t1 = t[..., :half].astype(jnp.float32)
    t2 = t[..., half:].astype(jnp.float32)
    return jnp.concatenate([t1 * cos - t2 * sin, t2 * cos + t1 * sin], axis=-1).astype(t.dtype)


def t5_bucket(rel):
    nb = T5_BUCKETS // 2
    max_exact = nb // 2
    bucket = jnp.where(rel > 0, nb, 0)
    n = jnp.abs(rel)
    large = max_exact + (jnp.log(jnp.maximum(n, 1).astype(jnp.float32) / max_exact)
                         / math.log(T5_MAX_DIST / max_exact) * (nb - max_exact)).astype(jnp.int32)
    large = jnp.minimum(large, nb - 1)
    return bucket + jnp.where(n < max_exact, n, large)


def block_causal_attention(q, k, v, scale, bias_fn, frame_causal):
    seq = q.shape[2]
    outs = []
    for q0 in range(0, seq, Q_BLOCK):
        kl = q0 + Q_BLOCK
        logits = jnp.einsum('bhqd,bhkd->bhqk', q[:, :, q0:kl], k[:, :, :kl]).astype(jnp.float32) * scale
        if bias_fn is not None:
            logits = logits + bias_fn(q0, kl)
        qi = jnp.arange(q0, kl)[:, None]
        ki = jnp.arange(kl)[None, :]
        allowed = (ki <= qi) if frame_causal else ((ki // CHUNK) <= (qi // CHUNK))
        logits = jnp.where(allowed, logits, -1e30)
        p = jax.nn.softmax(logits, axis=-1).astype(v.dtype)
        outs.append(jnp.einsum('bhqk,bhkd->bhqd', p, v[:, :, :kl]))
    return jnp.concatenate(outs, axis=2)


def hybrid_mixer(h, positions, t5_table, layer, w_in, q_norm_g, w_uq, kv_norm_g, w_ukv,
                 forget_b, conv_w, diff_lambda, subln_g, w_out):
    bsz, seq, _ = h.shape
    proj = jnp.einsum('bsd,dp->bsp', h, w_in)
    p_mla, p_fox, p_conv, p_diff = jnp.split(
        proj, [P_MLA, P_MLA + P_FOX, P_MLA + P_FOX + P_CONV], axis=-1)

    c_q, c_kv, k_rope = jnp.split(p_mla, [MLA_Q_LORA, MLA_Q_LORA + MLA_KV_LORA], axis=-1)
    q_a = jnp.einsum('bsr,rn->bsn', rms_norm(c_q, q_norm_g), w_uq).reshape(
        bsz, seq, MLA_HEADS, MLA_NOPE + MLA_ROPE)
    kv_a = jnp.einsum('bsr,rn->bsn', rms_norm(c_kv, kv_norm_g), w_ukv).reshape(
        bsz, seq, MLA_HEADS, MLA_NOPE + MLA_V)
    k_rope = rope(k_rope, positions)
    q_a = jnp.concatenate([q_a[..., :MLA_NOPE], rope(q_a[..., MLA_NOPE:], positions)], axis=-1)
    k_a = jnp.concatenate([kv_a[..., :MLA_NOPE],
                           jnp.broadcast_to(k_rope[:, :, None, :], (bsz, seq, MLA_HEADS, MLA_ROPE))], axis=-1)
    v_a = kv_a[..., MLA_NOPE:]
    out_a = block_causal_attention(q_a.transpose(0, 2, 1, 3), k_a.transpose(0, 2, 1, 3),
                                   v_a.transpose(0, 2, 1, 3), (MLA_NOPE + MLA_ROPE) ** -0.5, None, False)

    q_b, k_b, v_b, f_logit = jnp.split(p_fox, [GROUP_WIDTH, 2 * GROUP_WIDTH, 3 * GROUP_WIDTH], axis=-1)
    log_f = jax.nn.log_sigmoid(f_logit.astype(jnp.float32) + forget_b.astype(jnp.float32))
    cum_f = jnp.cumsum(log_f, axis=1).transpose(0, 2, 1)

    def forget_bias(q0, kl):
        return cum_f[:, :, q0:kl, None] - cum_f[:, :, None, :kl]

    out_b = block_causal_attention(to_heads(q_b, FOX_HEADS), to_heads(k_b, FOX_HEADS),
                                   to_heads(v_b, FOX_HEADS), HEAD_DIM ** -0.5, forget_bias, True)

    b_gate, c_gate, u = jnp.split(p_conv, [CONV_CH, 2 * CONV_CH], axis=-1)
    z = jnp.pad(c_gate * u, ((0, 0), (CONV_WIDTH - 1, 0), (0, 0)))
    conv = z[:, CONV_WIDTH - 1:] * conv_w[CONV_WIDTH - 1]
    for tap in range(CONV_WIDTH - 1):
        conv = conv + z[:, tap:tap + seq] * conv_w[tap]
    out_c = b_gate * conv

    q_d, k_d, v_d = jnp.split(p_diff, [GROUP_WIDTH, 2 * GROUP_WIDTH], axis=-1)

    def two_maps(t):
        t = t.reshape(bsz, seq, DIFF_HEADS, 2, DIFF_HALF).transpose(0, 3, 2, 1, 4)
        return t.reshape(bsz, 2 * DIFF_HEADS, seq, DIFF_HALF)

    v_d = to_heads(v_d, DIFF_HEADS)
    lam_init = 0.8 - 0.6 * math.exp(-0.3 * layer)
    lam_par = diff_lambda.astype(jnp.float32)
    lam = jnp.exp(jnp.sum(lam_par[0] * lam_par[1])) - jnp.exp(jnp.sum(lam_par[2] * lam_par[3])) + lam_init

    def t5_bias(q0, kl):
        rel = positions[:, None, :kl] - positions[:, q0:kl, None]
        b = t5_table[t5_bucket(rel)].astype(jnp.float32).transpose(0, 3, 1, 2)
        return jnp.concatenate([b, b], axis=1)

    o_d = block_causal_attention(two_maps(q_d), two_maps(k_d), jnp.concatenate([v_d, v_d], axis=1),
                                 DIFF_HALF ** -0.5, t5_bias, False)
    o_d = o_d[:, :DIFF_HEADS] - lam * o_d[:, DIFF_HEADS:]
    out_d = (rms_norm(o_d, subln_g) * (1.0 - lam_init)).astype(h.dtype)

    mixed = jnp.concatenate([from_heads(out_a), from_heads(out_b), out_c, from_heads(out_d)], axis=-1)
    return jnp.einsum('bsm,md->bsd', mixed, w_out)


def hierarchical_moe(h, rg_w, rg_b, re_w, re_b, w_gate, w_up, w_down):
    bsz, seq, d = h.shape
    xs = h.reshape(-1, d)
    n_tok = xs.shape[0]
    g_logits = jnp.einsum('nd,dg->ng', xs, rg_w).astype(jnp.float32) + rg_b.astype(jnp.float32)
    g_prob = jax.nn.softmax(g_logits, axis=-1)
    grp = jnp.argmax(g_logits, axis=-1).astype(jnp.int32)
    p_grp = jnp.take_along_axis(g_prob, grp[:, None], axis=-1)
    e_logits = (jnp.einsum('nd,de->ne', xs, re_w).astype(jnp.float32) + re_b.astype(jnp.float32)).reshape(
        n_tok, N_GROUPS, EXPERTS_PER_GROUP)
    e_logits = jnp.take_along_axis(e_logits, grp[:, None, None], axis=1)[:, 0]
    top_p, top_i = lax.top_k(jax.nn.softmax(e_logits, axis=-1), TOP_K)
    gates = p_grp * top_p / jnp.sum(top_p, axis=-1, keepdims=True)
    expert = grp[:, None] * EXPERTS_PER_GROUP + top_i.astype(jnp.int32)

    n_assign = n_tok * TOP_K
    e_flat = expert.reshape(n_assign)
    tok_flat = jnp.arange(n_assign, dtype=jnp.int32) // TOP_K
    order = jnp.argsort(e_flat)
    e_sorted = e_flat[order]
    counts = jnp.zeros((N_EXPERTS,), jnp.int32).at[e_flat].add(1)
    starts = jnp.cumsum(counts) - counts
    padded = (counts + MOE_BLOCK - 1) // MOE_BLOCK * MOE_BLOCK
    pad_ends = jnp.cumsum(padded)
    pad_starts = pad_ends - padded
    dest_sorted = pad_starts[e_sorted] + (jnp.arange(n_assign, dtype=jnp.int32) - starts[e_sorted])
    n_blocks = (n_assign + N_EXPERTS * (MOE_BLOCK - 1) + MOE_BLOCK - 1) // MOE_BLOCK
    n_slots = n_blocks * MOE_BLOCK
    slot_tok = jnp.full((n_slots,), n_tok, jnp.int32).at[dest_sorted].set(tok_flat[order])
    xs_pad = jnp.concatenate([xs, jnp.zeros((1, d), xs.dtype)], axis=0)
    x_slots = xs_pad[slot_tok].reshape(n_blocks, MOE_BLOCK, d)
    block_expert = jnp.minimum(
        jnp.searchsorted(pad_ends, jnp.arange(n_blocks, dtype=jnp.int32) * MOE_BLOCK, side='right'),
        N_EXPERTS - 1)

    def expert_block(args):
        xb, e = args
        hid = jax.nn.silu(xb @ w_gate[e]) * (xb @ w_up[e])
        return hid @ w_down[e]

    y_slots = lax.map(expert_block, (x_slots, block_expert)).reshape(n_slots, d)
    dest = jnp.zeros((n_assign,), jnp.int32).at[order].set(dest_sorted).reshape(n_tok, TOP_K)
    y = jnp.einsum('nk,nkd->nd', gates.astype(xs.dtype), y_slots[dest])
    return y.reshape(bsz, seq, d)


def setup_inputs(seed: int = 0) -> dict:
    key = jax.random.key(seed)
    ks = jax.random.split(key, 26)
    f32 = jnp.float32

    def nrm(k, shape, scale):
        return jax.random.normal(k, shape, f32) * scale

    def gain(k, shape):
        return 1.0 + 0.01 * jax.random.normal(k, shape, f32)

    offset = jax.random.randint(ks[2], (BATCH,), 0, 16) * CHUNK
    positions = (offset[:, None] + jnp.arange(SEQ)[None, :]).astype(jnp.int32)
    return {
        'x': nrm(ks[0], (BATCH, SEQ, D_MODEL), 1.0),
        'c': nrm(ks[1], (BATCH, D_MODEL), 1.0),
        'positions': positions,
        't5_table': nrm(ks[3], (T5_BUCKETS, DIFF_HEADS), 0.5),
        'ada_w': nrm(ks[4], (DEPTH, D_MODEL, ADA_CHUNKS * D_MODEL), 0.5 * D_MODEL ** -0.5),
        'ada_b': nrm(ks[5], (DEPTH, ADA_CHUNKS * D_MODEL), 0.02),
        'norm_mix_g': gain(ks[6], (DEPTH, D_MODEL)),
        'norm_ffn_g': gain(ks[7], (DEPTH, D_MODEL)),
        'w_in': nrm(ks[8], (DEPTH, D_MODEL, P_IN), D_MODEL ** -0.5),
        'mla_q_norm_g': gain(ks[9], (DEPTH, MLA_Q_LORA)),
        'mla_w_uq': nrm(ks[10], (DEPTH, MLA_Q_LORA, MLA_HEADS * (MLA_NOPE + MLA_ROPE)), MLA_Q_LORA ** -0.5),
        'mla_kv_norm_g': gain(ks[11], (DEPTH, MLA_KV_LORA)),
        'mla_w_ukv': nrm(ks[12], (DEPTH, MLA_KV_LORA, MLA_HEADS * (MLA_NOPE + MLA_V)), MLA_KV_LORA ** -0.5),
        'fox_forget_b': 2.0 + 0.5 * jax.random.normal(ks[13], (DEPTH, FOX_HEADS), f32),
        'conv_w': nrm(ks[14], (DEPTH, CONV_WIDTH, CONV_CH), CONV_WIDTH ** -0.5),
        'diff_lambda': nrm(ks[15], (DEPTH, 4, DIFF_HALF), 0.1),
        'diff_subln_g': gain(ks[16], (DEPTH, HEAD_DIM)),
        'w_out': nrm(ks[17], (DEPTH, D_MIX, D_MODEL), D_MIX ** -0.5),
        'router_group_w': nrm(ks[18], (DEPTH, D_MODEL, N_GROUPS), D_MODEL ** -0.5),
        'router_group_b': nrm(ks[19], (DEPTH, N_GROUPS), 0.01),
        'router_expert_w': nrm(ks[20], (DEPTH, D_MODEL, N_EXPERTS), D_MODEL ** -0.5),
        'router_expert_b': nrm(ks[21], (DEPTH, N_EXPERTS), 0.01),
        'expert_w_gate': nrm(ks[22], (DEPTH, N_EXPERTS, D_MODEL, D_EXPERT), D_MODEL ** -0.5),
        'expert_w_up': nrm(ks[23], (DEPTH, N_EXPERTS, D_MODEL, D_EXPERT), D_MODEL ** -0.5),
        'expert_w_down': nrm(ks[24], (DEPTH, N_EXPERTS, D_EXPERT, D_MODEL), D_EXPERT ** -0.5),
        'final_norm_g': gain(ks[25], (D_MODEL,)),
    }


def reference(x, c, positions, t5_table, ada_w, ada_b, norm_mix_g, norm_ffn_g, w_in,
              mla_q_norm_g, mla_w_uq, mla_kv_norm_g, mla_w_ukv, fox_forget_b, conv_w,
              diff_lambda, diff_subln_g, w_out, router_group_w, router_group_b,
              router_expert_w, router_expert_b, expert_w_gate, expert_w_up, expert_w_down,
              final_norm_g):
    cond = jax.nn.silu(c)
    for layer in range(DEPTH):
        mod = jnp.einsum('bd,dm->bm', cond, ada_w[layer]) + ada_b[layer]
        sh_m, sc_m, g_m, sh_f, sc_f, g_f = [m[:, None, :] for m in jnp.split(mod, ADA_CHUNKS, axis=-1)]
        h = rms_norm(x, norm_mix_g[layer]) * (1.0 + sc_m) + sh_m
        x = x + g_m * hybrid_mixer(h, positions, t5_table, layer, w_in[layer],
                                   mla_q_norm_g[layer], mla_w_uq[layer], mla_kv_norm_g[layer],
                                   mla_w_ukv[layer], fox_forget_b[layer], conv_w[layer],
                                   diff_lambda[layer], diff_subln_g[layer], w_out[layer])
        h = rms_norm(x, norm_ffn_g[layer]) * (1.0 + sc_f) + sh_f
        x = x + g_f * hierarchical_moe(h, router_group_w[layer], router_group_b[layer],
                                       router_expert_w[layer], router_expert_b[layer],
                                       expert_w_gate[layer], expert_w_up[layer], expert_w_down[layer])
    return rms_norm(x, final_norm_g)
```

```python
import functools
import math

import jax
import jax.numpy as jnp
from jax import lax
from jax.experimental import pallas as pl
from jax.experimental.pallas import tpu as pltpu

F32 = jnp.float32
BF16 = jnp.bfloat16

D_MODEL = 1024
DEPTH = 4
CHUNK = 64
EPS = 1e-6
HEAD_DIM = 64
GROUP_WIDTH = 256
MLA_NOPE = 64
MLA_ROPE = 32
MLA_Q_LORA = 192
MLA_KV_LORA = 128
ROPE_THETA = 10000.0
DIFF_HALF = 32
T5_BUCKETS = 32
T5_MAX_DIST = 128
N_GROUPS = 8
EXPERTS_PER_GROUP = 8
N_EXPERTS = 64
TOP_K = 2
D_EXPERT = 512
ADA_CHUNKS = 6

LANES = 128
LOG2E = 1.4426950408889634
NEG = -1e30

_O_MLA = 0
_O_FOX = MLA_Q_LORA + MLA_KV_LORA + MLA_ROPE
_O_CONV = _O_FOX + 3 * GROUP_WIDTH + 4
_O_DIFF = _O_CONV + 3 * GROUP_WIDTH

_CQ_PAD = 256
_W_A = _CQ_PAD + MLA_KV_LORA + 2 * LANES
_W_B = 3 * GROUP_WIDTH
_W_C = 3 * GROUP_WIDTH
_W_D = 3 * GROUP_WIDTH
_P_TOTAL = _W_A + _W_B + _W_C + _W_D

ROW_TILE = 512
ATT_TILE = 256
MOE_TILE = 256
CMB_TILE = 256
VMEM_LIMIT = 56 * 1024 * 1024


def _cparams(sem):
    return pltpu.CompilerParams(dimension_semantics=sem, vmem_limit_bytes=VMEM_LIMIT)


def _split_bf16(a):
    hi = a.astype(BF16)
    lo = (a - hi.astype(F32)).astype(BF16)
    return hi, lo


def _dot(a, b):
    return jnp.dot(a, b, preferred_element_type=F32)


def _dot_nt(a, b):
    return lax.dot_general(a, b, (((1,), (1,)), ((), ())), preferred_element_type=F32)


def _sigmoid(z):
    return 1.0 / (1.0 + jnp.exp(-z))


def _ada_kernel(c_ref, w_ref, b_ref, o_ref):
    c = c_ref[...]
    cond = c * _sigmoid(c)
    c_hi, c_lo = _split_bf16(cond)
    w_hi, w_lo = _split_bf16(w_ref[0])
    o_ref[0] = _dot(c_hi, w_hi) + _dot(c_hi, w_lo) + _dot(c_lo, w_hi) + b_ref[0]


def _ada_call(c, ada_w, ada_b):
    bsz = c.shape[0]
    n_col = ADA_CHUNKS * D_MODEL // D_MODEL
    return pl.pallas_call(
        _ada_kernel,
        grid=(DEPTH, n_col),
        in_specs=[
            pl.BlockSpec((bsz, D_MODEL), lambda l, j: (0, 0)),
            pl.BlockSpec((1, D_MODEL, D_MODEL), lambda l, j: (l, 0, j)),
            pl.BlockSpec((1, 1, D_MODEL), lambda l, j: (l, 0, j)),
        ],
        out_specs=pl.BlockSpec((1, bsz, D_MODEL), lambda l, j: (l, 0, j)),
        out_shape=jax.ShapeDtypeStruct((DEPTH, bsz, ADA_CHUNKS * D_MODEL), F32),
        compiler_params=_cparams(("arbitrary", "arbitrary")),
        name="ada_mod",
    )(c, ada_w, ada_b.reshape(DEPTH, 1, ADA_CHUNKS * D_MODEL))


def _proj_kernel(x_ref, mod_ref, g_ref, w_ref, wf_ref, fb_ref, wuq_ref, wukv_ref, qg_ref, kvg_ref,
                 cos_ref, sin_ref, cw_ref,
                 qa_ref, ka_ref, va_ref, qb_ref, kb_ref, vb_ref, nf_ref, oc_ref, qd_ref, kd_ref, vd_ref,
                 zc_ref, fc_ref, *, scale_a, scale_b, scale_d):
    t = pl.program_id(1)
    rows = x_ref.shape[0]

    @pl.when(t == 0)
    def _():
        zc_ref[...] = jnp.zeros_like(zc_ref)
        fc_ref[...] = jnp.zeros_like(fc_ref)

    x = x_ref[...]
    mod = mod_ref[0]
    hn = x * lax.rsqrt(jnp.mean(x * x, axis=-1, keepdims=True) + EPS) * g_ref[...]
    h = hn * (1.0 + mod[1:2]) + mod[0:1]
    hb = h.astype(BF16)

    z = _dot_nt(wf_ref[...], hb) + fb_ref[...]
    log_f = jnp.minimum(z, 0.0) - jnp.log(1.0 + jnp.exp(-jnp.abs(z)))
    r_i = lax.broadcasted_iota(jnp.int32, (rows, rows), 0)
    c_i = lax.broadcasted_iota(jnp.int32, (rows, rows), 1)
    tri = jnp.where(r_i <= c_i, 1.0, 0.0).astype(BF16)
    f_hi = log_f.astype(BF16)
    rem = log_f - f_hi.astype(F32)
    f_mid = rem.astype(BF16)
    f_lo = (rem - f_mid.astype(F32)).astype(BF16)
    cum = _dot(f_hi, tri) + _dot(f_mid, tri) + _dot(f_lo, tri) + fc_ref[:, 0:1]
    fc_ref[...] = jnp.broadcast_to(cum[:, rows - 1:rows], fc_ref.shape)
    nf_ref[0] = cum[0:8] * (-LOG2E)

    pa = _dot(hb, w_ref[:, 0:_W_A])
    cq = pa[:, 0:_CQ_PAD]
    ckv = pa[:, _CQ_PAD:_CQ_PAD + MLA_KV_LORA]
    kr = pa[:, _CQ_PAD + MLA_KV_LORA:_CQ_PAD + MLA_KV_LORA + LANES]
    krr = pa[:, _CQ_PAD + MLA_KV_LORA + LANES:_W_A]
    cqn = cq * lax.rsqrt(jnp.sum(cq * cq, axis=-1, keepdims=True) * (1.0 / MLA_Q_LORA) + EPS) * qg_ref[...]
    ckvn = ckv * lax.rsqrt(jnp.mean(ckv * ckv, axis=-1, keepdims=True) + EPS) * kvg_ref[...]
    q2 = _dot(cqn.astype(BF16), wuq_ref[...])
    kv2 = _dot(ckvn.astype(BF16), wukv_ref[...])
    cos = cos_ref[...]
    sin = sin_ref[...]
    cos4 = jnp.concatenate([cos] * 4, axis=-1)
    sin4 = jnp.concatenate([sin] * 4, axis=-1)
    qa = (q2[:, 0:512] * cos4 + q2[:, 512:1024] * sin4) * (scale_a * LOG2E)
    k_rope = kr * cos + krr * sin
    ka = kv2[:, 0:512] + jnp.concatenate([k_rope] * 4, axis=-1)
    qa_ref[...] = qa.astype(BF16)
    ka_ref[...] = ka.astype(BF16)
    va_ref[...] = kv2[:, 512:768].astype(BF16)

    pb = _dot(hb, w_ref[:, _W_A:_W_A + _W_B])
    qb_ref[...] = (pb[:, 0:256] * (scale_b * LOG2E)).astype(BF16)
    kb_ref[...] = pb[:, 256:512].astype(BF16)
    vb_ref[...] = pb[:, 512:768].astype(BF16)

    pc = _dot(hb, w_ref[:, _W_A + _W_B:_W_A + _W_B + _W_C])
    zz = pc[:, 256:512] * pc[:, 512:768]
    ext = jnp.concatenate([zc_ref[...], zz], axis=0)
    cw = cw_ref[...]
    conv = zz * cw[2:3] + ext[7:rows + 7] * cw[1:2] + ext[6:rows + 6] * cw[0:1]
    zc_ref[...] = zz[rows - 8:rows]
    oc_ref[...] = (pc[:, 0:256] * conv).astype(BF16)

    pd = _dot(hb, w_ref[:, _W_A + _W_B + _W_C:_P_TOTAL])
    qd_ref[...] = (pd[:, 0:256] * (scale_d * LOG2E)).astype(BF16)
    kd_ref[...] = pd[:, 256:512].astype(BF16)
    vd_ref[...] = pd[:, 512:768].astype(BF16)


def _proj_call(x, mod, g, w, wf, fb, wuq, wukv, qg, kvg, cos_t, sin_t, cw, bsz, seq):
    n = bsz * seq
    rt = min(ROW_TILE, seq)
    nt = seq // rt
    row = lambda width: pl.BlockSpec((rt, width), lambda b, t: (b * nt + t, 0))
    full = lambda a: pl.BlockSpec(a.shape, lambda b, t: (0,) * a.ndim)
    bf = lambda width: jax.ShapeDtypeStruct((n, width), BF16)
    kern = functools.partial(
        _proj_kernel,
        scale_a=(MLA_NOPE + MLA_ROPE) ** -0.5, scale_b=HEAD_DIM ** -0.5, scale_d=DIFF_HALF ** -0.5)
    return pl.pallas_call(
        kern,
        grid=(bsz, nt),
        in_specs=[
            row(D_MODEL),
            pl.BlockSpec((1, ADA_CHUNKS, D_MODEL), lambda b, t: (b, 0, 0)),
            full(g), full(w), full(wf), full(fb), full(wuq), full(wukv), full(qg), full(kvg),
            row(LANES), row(LANES), full(cw),
        ],
        out_specs=[
            row(512), row(512), row(256), row(256), row(256), row(256),
            pl.BlockSpec((1, 8, rt), lambda b, t: (b, 0, t)),
            row(256), row(256), row(256), row(256),
        ],
        out_shape=[
            bf(512), bf(512), bf(256), bf(256), bf(256), bf(256),
            jax.ShapeDtypeStruct((bsz, 8, seq), F32),
            bf(256), bf(256), bf(256), bf(256),
        ],
        scratch_shapes=[pltpu.VMEM((8, GROUP_WIDTH), F32), pltpu.VMEM((16, LANES), F32)],
        compiler_params=_cparams(("arbitrary", "arbitrary")),
        name="norm_in_proj",
    )(x, mod, g, w, wf, fb, wuq, wukv, qg, kvg, cos_t, sin_t, cw)


def _tile_update(qm, k_t, v_t, m_ref, l_ref, acc_ref, idx, bias=None, mask=None):
    s = _dot_nt(qm, k_t)
    if bias is not None:
        s = s + bias
    if mask is not None:
        s = jnp.where(mask, s, NEG)
    m_prev = m_ref[idx]
    m_new = jnp.maximum(m_prev, jnp.max(s, axis=-1, keepdims=True))
    alpha = jnp.exp2(m_prev - m_new)
    p = jnp.exp2(s - m_new)
    l_ref[idx] = alpha * l_ref[idx] + jnp.sum(p, axis=-1, keepdims=True)
    acc_ref[idx] = alpha * acc_ref[idx] + _dot(p.astype(BF16), v_t)
    m_ref[idx] = m_new


def _attn_kernel(*refs, kind, tile):
    if kind == "A":
        q_ref, k_ref, v_ref, o_ref, m_ref, l_ref, acc_ref = refs
    elif kind == "B":
        q_ref, k_ref, v_ref, nf_ref, o_ref, m_ref, l_ref, acc_ref = refs
    else:
        q_ref, k_ref, v_ref, bias_ref, lam_ref, sg_ref, o_ref, m_ref, l_ref, acc_ref = refs
    i = pl.program_id(1)
    lane = lax.broadcasted_iota(jnp.int32, (1, LANES), 1)
    r_i = lax.broadcasted_iota(jnp.int32, (tile, tile), 0)
    c_i = lax.broadcasted_iota(jnp.int32, (tile, tile), 1)
    if kind == "B":
        diag_mask = c_i <= r_i
    else:
        shift = CHUNK.bit_length() - 1
        diag_mask = jnp.right_shift(c_i, shift) <= jnp.right_shift(r_i, shift)

    def run_map(idx, qm, k_slab, v_slab, head):
        m_ref[idx] = jnp.full(m_ref.shape[1:], NEG, F32)
        l_ref[idx] = jnp.zeros(l_ref.shape[1:], F32)
        acc_ref[idx] = jnp.zeros(acc_ref.shape[1:], F32)

        def kv(j):
            start = pl.multiple_of(j * tile, tile)
            k_t = k_ref[pl.ds(start, tile), k_slab * LANES:(k_slab + 1) * LANES]
            v_t = v_ref[pl.ds(start, tile), v_slab * LANES:(v_slab + 1) * LANES]
            return k_t, v_t

        def far(j, carry):
            k_t, v_t = kv(j)
            bias = nf_ref[0, j][head:head + 1, :] if kind == "B" else None
            _tile_update(qm, k_t, v_t, m_ref, l_ref, acc_ref, idx, bias=bias)
            return carry

        n_far = jnp.maximum(i - 1, 0) if kind == "D" else i
        lax.fori_loop(0, n_far, far, 0)

        if kind == "D":
            @pl.when(i >= 1)
            def _():
                k_t, v_t = kv(i - 1)
                _tile_update(qm, k_t, v_t, m_ref, l_ref, acc_ref, idx, bias=bias_ref[head, 1])

        k_t, v_t = kv(i)
        if kind == "A":
            bias = None
        elif kind == "B":
            bias = nf_ref[0, i][head:head + 1, :]
        else:
            bias = bias_ref[head, 0]
        _tile_update(qm, k_t, v_t, m_ref, l_ref, acc_ref, idx, bias=bias, mask=diag_mask)
        return acc_ref[idx] / l_ref[idx]

    low_half = lane < HEAD_DIM
    for slab in range(2):
        outs = []
        for sub in range(2):
            head = 2 * slab + sub
            if kind == "A":
                o = run_map(0, q_ref[:, head * LANES:(head + 1) * LANES], head, slab, head)
            elif kind == "B":
                q = q_ref[:, slab * LANES:(slab + 1) * LANES]
                qm = jnp.where(low_half if sub == 0 else jnp.logical_not(low_half), q, jnp.zeros_like(q))
                o = run_map(0, qm, slab, slab, head)
            else:
                q = q_ref[:, slab * LANES:(slab + 1) * LANES]
                o_maps = []
                for mp in range(2):
                    lo = (2 * sub + mp) * DIFF_HALF
                    sel = (lane >= lo) & (lane < lo + DIFF_HALF)
                    qm = jnp.where(sel, q, jnp.zeros_like(q))
                    o_maps.append(run_map(mp, qm, slab, slab, head))
                o = o_maps[0] - lam_ref[...] * o_maps[1]
            outs.append(o)
        o_slab = jnp.where(low_half, outs[0], outs[1])
        if kind == "D":
            sq = o_slab * o_slab
            s_lo = jnp.sum(jnp.where(low_half, sq, 0.0), axis=-1, keepdims=True)
            s_hi = jnp.sum(jnp.where(low_half, 0.0, sq), axis=-1, keepdims=True)
            ms = jnp.where(low_half, s_lo, s_hi) * (1.0 / HEAD_DIM)
            o_slab = o_slab * lax.rsqrt(ms + EPS) * sg_ref[...]
        o_ref[:, slab * LANES:(slab + 1) * LANES] = o_slab.astype(BF16)


def _attn_call(kind, q, k, v, extra, bsz, seq):
    n = bsz * seq
    tile = min(ATT_TILE, seq)
    nq = seq // tile
    wq = q.shape[1]
    in_specs = [
        pl.BlockSpec((tile, wq), lambda b, i: (b * nq + i, 0)),
        pl.BlockSpec((seq, k.shape[1]), lambda b, i: (b, 0)),
        pl.BlockSpec((seq, v.shape[1]), lambda b, i: (b, 0)),
    ]
    if kind == "B":
        in_specs.append(pl.BlockSpec((1, nq, 8, tile), lambda b, i: (b, 0, 0, 0)))
    elif kind == "D":
        bias, lam, sg = extra
        in_specs += [
            pl.BlockSpec(bias.shape, lambda b, i: (0, 0, 0, 0)),
            pl.BlockSpec(lam.shape, lambda b, i: (0, 0)),
            pl.BlockSpec(sg.shape, lambda b, i: (0, 0)),
        ]
    n_state = 2 if kind == "D" else 1
    return pl.pallas_call(
        functools.partial(_attn_kernel, kind=kind, tile=tile),
        grid=(bsz, nq),
        in_specs=in_specs,
        out_specs=pl.BlockSpec((tile, GROUP_WIDTH), lambda b, i: (b * nq + i, 0)),
        out_shape=jax.ShapeDtypeStruct((n, GROUP_WIDTH), BF16),
        scratch_shapes=[
            pltpu.VMEM((n_state, tile, 1), F32),
            pltpu.VMEM((n_state, tile, 1), F32),
            pltpu.VMEM((n_state, tile, LANES), F32),
        ],
        compiler_params=_cparams(("arbitrary", "arbitrary")),
        name="attn_" + kind,
    )(q, k, v, *extra)


def _out_router_kernel(oa_ref, ob_ref, oc_ref, od_ref, wo_ref, x_ref, mod_ref, g_ref, wr_hi_ref, wr_lo_ref,
                       rb_ref, xo_ref, h2_ref, route_ref):
    gw = GROUP_WIDTH
    mix = (_dot(oa_ref[...], wo_ref[0:gw]) + _dot(ob_ref[...], wo_ref[gw:2 * gw])
           + _dot(oc_ref[...], wo_ref[2 * gw:3 * gw]) + _dot(od_ref[...], wo_ref[3 * gw:4 * gw]))
    mod = mod_ref[0]
    x1 = x_ref[...] + mod[2:3] * mix
    xo_ref[...] = x1
    hn = x1 * lax.rsqrt(jnp.mean(x1 * x1, axis=-1, keepdims=True) + EPS) * g_ref[...]
    h2 = hn * (1.0 + mod[4:5]) + mod[3:4]
    h2_ref[...] = h2

    a_hi, a_lo = _split_bf16(h2)
    w_hi = wr_hi_ref[...]
    lg = _dot_nt(w_hi, a_hi) + _dot_nt(w_hi, a_lo) + _dot_nt(wr_lo_ref[...], a_hi) + rb_ref[...]
    gl = lg[0:N_GROUPS]
    rows = gl.shape[1]
    iota8 = lax.broadcasted_iota(jnp.int32, (N_GROUPS, rows), 0)
    gmax = jnp.max(gl, axis=0, keepdims=True)
    grp = jnp.min(jnp.where(gl == gmax, iota8, N_GROUPS), axis=0, keepdims=True)
    p_grp = 1.0 / jnp.sum(jnp.exp(gl - gmax), axis=0, keepdims=True)
    esel = jnp.zeros((EXPERTS_PER_GROUP, rows), F32)
    for gi in range(N_GROUPS):
        lo = N_GROUPS + gi * EXPERTS_PER_GROUP
        esel = jnp.where(grp == gi, lg[lo:lo + EXPERTS_PER_GROUP], esel)
    ee = jnp.exp(esel - jnp.max(esel, axis=0, keepdims=True))
    ps = ee / jnp.sum(ee, axis=0, keepdims=True)
    p1 = jnp.max(ps, axis=0, keepdims=True)
    i1 = jnp.min(jnp.where(ps == p1, iota8, EXPERTS_PER_GROUP), axis=0, keepdims=True)
    ps2 = jnp.where(iota8 == i1, -1.0, ps)
    p2 = jnp.max(ps2, axis=0, keepdims=True)
    i2 = jnp.min(jnp.where(ps2 == p2, iota8, EXPERTS_PER_GROUP), axis=0, keepdims=True)
    den = p1 + p2
    g1 = p_grp * p1 / den
    g2 = p_grp * p2 / den
    e1 = (grp * EXPERTS_PER_GROUP + i1).astype(F32)
    e2 = (grp * EXPERTS_PER_GROUP + i2).astype(F32)
    zero = jnp.zeros_like(g1)
    route_ref[...] = jnp.concatenate([g1, g2, e1, e2, zero, zero, zero, zero], axis=0)


def _out_router_call(oa, ob, oc, od, wo, x, mod, g, wr_hi, wr_lo, rb, bsz, seq):
    n = bsz * seq
    rt = min(ROW_TILE, seq)
    nt = seq // rt
    row = lambda width: pl.BlockSpec((rt, width), lambda t: (t, 0))
    full = lambda a: pl.BlockSpec(a.shape, lambda t: (0,) * a.ndim)
    return pl.pallas_call(
        _out_router_kernel,
        grid=(n // rt,),
        in_specs=[
            row(GROUP_WIDTH), row(GROUP_WIDTH), row(GROUP_WIDTH), row(GROUP_WIDTH), full(wo),
            row(D_MODEL),
            pl.BlockSpec((1, ADA_CHUNKS, D_MODEL), lambda t: (t // nt, 0, 0)),
            full(g), full(wr_hi), full(wr_lo), full(rb),
        ],
        out_specs=[row(D_MODEL), row(D_MODEL), pl.BlockSpec((8, rt), lambda t: (0, t))],
        out_shape=[
            jax.ShapeDtypeStruct((n, D_MODEL), F32),
            jax.ShapeDtypeStruct((n, D_MODEL), F32),
            jax.ShapeDtypeStruct((8, n), F32),
        ],
        compiler_params=_cparams(("arbitrary",)),
        name="out_proj_router",
    )(oa, ob, oc, od, wo, x, mod, g, wr_hi, wr_lo, rb)


def _row_gather(idx_of, src_hbm, dst, sem, n_rows):
    def issue(r, carry):
        pltpu.make_async_copy(src_hbm.at[pl.ds(idx_of(r), 1)], dst.at[pl.ds(r, 1)], sem).start()
        return carry
    lax.fori_loop(0, n_rows, issue, 0)


def _expert_kernel(be_ref, nact_ref, tok_ref, h_hbm, wg_ref, wu_ref, wd_ref, y_ref,
                   xbuf, wgb, wub, wdb, sem):
    i = pl.program_id(0)
    rows = xbuf.shape[0]

    @pl.when(i < nact_ref[0])
    def _():
        _row_gather(lambda r: tok_ref[0, 0, r], h_hbm, xbuf, sem, rows)
        prev = be_ref[jnp.maximum(i - 1, 0)]

        @pl.when((i == 0) | (be_ref[i] != prev))
        def _():
            wgb[...] = wg_ref[0].astype(BF16)
            wub[...] = wu_ref[0].astype(BF16)
            wdb[...] = wd_ref[0].astype(BF16)

        pltpu.make_async_copy(h_hbm.at[pl.ds(0, rows)], xbuf, sem).wait()
        xb = xbuf[...].astype(BF16)
        gate = _dot(xb, wgb[...])
        up = _dot(xb, wub[...])
        hid = gate * _sigmoid(gate) * up
        y_ref[...] = _dot(hid.astype(BF16), wdb[...])

    @pl.when(i >= nact_ref[0])
    def _():
        y_ref[...] = jnp.zeros_like(y_ref)


def _expert_call(block_expert, n_active, slot_tok, h2, w_gate, w_up, w_down):
    n_blocks = block_expert.shape[0]
    tm = MOE_TILE
    grid_spec = pltpu.PrefetchScalarGridSpec(
        num_scalar_prefetch=2,
        grid=(n_blocks,),
        in_specs=[
            pl.BlockSpec((1, 1, tm), lambda i, be, na: (i, 0, 0), memory_space=pltpu.SMEM),
            pl.BlockSpec(memory_space=pl.ANY),
            pl.BlockSpec((1, D_MODEL, D_EXPERT), lambda i, be, na: (be[i], 0, 0)),
            pl.BlockSpec((1, D_MODEL, D_EXPERT), lambda i, be, na: (be[i], 0, 0)),
            pl.BlockSpec((1, D_EXPERT, D_MODEL), lambda i, be, na: (be[i], 0, 0)),
        ],
        out_specs=pl.BlockSpec((tm, D_MODEL), lambda i, be, na: (i, 0)),
        scratch_shapes=[
            pltpu.VMEM((tm, D_MODEL), F32),
            pltpu.VMEM((D_MODEL, D_EXPERT), BF16),
            pltpu.VMEM((D_MODEL, D_EXPERT), BF16),
            pltpu.VMEM((D_EXPERT, D_MODEL), BF16),
            pltpu.SemaphoreType.DMA,
        ],
    )
    return pl.pallas_call(
        _expert_kernel,
        grid_spec=grid_spec,
        out_shape=jax.ShapeDtypeStruct((n_blocks * tm, D_MODEL), F32),
        compiler_params=_cparams(("arbitrary",)),
        name="expert_mlp",
    )(block_expert, n_active, slot_tok.reshape(n_blocks, 1, tm), h2, w_gate, w_up, w_down)


def _combine_kernel(dest_ref, y_hbm, x_ref, gate_ref, mod_ref, fg_ref, o_ref, ybuf, sem, *, final):
    rows = x_ref.shape[0]
    for k in range(TOP_K):
        _row_gather(lambda r, k=k: dest_ref[0, 0, k * rows + r], y_hbm, ybuf.at[k], sem.at[k], rows)
    for k in range(TOP_K):
        pltpu.make_async_copy(y_hbm.at[pl.ds(0, rows)], ybuf.at[k], sem.at[k]).wait()
    gt = gate_ref[...]
    y = gt[:, 0:1] * ybuf[0] + gt[:, 1:2] * ybuf[1]
    x2 = x_ref[...] + mod_ref[0][5:6] * y
    if final:
        x2 = x2 * lax.rsqrt(jnp.mean(x2 * x2, axis=-1, keepdims=True) + EPS) * fg_ref[...]
    o_ref[...] = x2


def _combine_call(dest, y_slots, x, gates, mod, fg, bsz, seq, final):
    n = bsz * seq
    tc = min(CMB_TILE, seq)
    nt = seq // tc
    return pl.pallas_call(
        functools.partial(_combine_kernel, final=final),
        grid=(n // tc,),
        in_specs=[
            pl.BlockSpec((1, 1, TOP_K * tc), lambda t: (t, 0, 0), memory_space=pltpu.SMEM),
            pl.BlockSpec(memory_space=pl.ANY),
            pl.BlockSpec((tc, D_MODEL), lambda t: (t, 0)),
            pl.BlockSpec((tc, 8), lambda t: (t, 0)),
            pl.BlockSpec((1, ADA_CHUNKS, D_MODEL), lambda t: (t // nt, 0, 0)),
            pl.BlockSpec((1, D_MODEL), lambda t: (0, 0)),
        ],
        out_specs=pl.BlockSpec((tc, D_MODEL), lambda t: (t, 0)),
        out_shape=jax.ShapeDtypeStruct((n, D_MODEL), F32),
        scratch_shapes=[pltpu.VMEM((TOP_K, tc, D_MODEL), F32), pltpu.SemaphoreType.DMA((TOP_K,))],
        compiler_params=_cparams(("arbitrary",)),
        name="moe_combine",
    )(dest, y_slots, x, gates, mod, fg)


def _t5_bucket(rel):
    nb = T5_BUCKETS // 2
    max_exact = nb // 2
    bucket = jnp.where(rel > 0, nb, 0)
    n = jnp.abs(rel)
    large = max_exact + (jnp.log(jnp.maximum(n, 1).astype(F32) / max_exact)
                         / math.log(T5_MAX_DIST / max_exact) * (nb - max_exact)).astype(jnp.int32)
    large = jnp.minimum(large, nb - 1)
    return bucket + jnp.where(n < max_exact, n, large)


def _t5_tiles(t5_table, tile):
    r = jnp.arange(tile, dtype=jnp.int32)[:, None]
    c = jnp.arange(tile, dtype=jnp.int32)[None, :]
    rel = jnp.stack([c - r, c - r - tile])
    far = t5_table[_t5_bucket(jnp.array(-2 * tile, jnp.int32))]
    b = t5_table[_t5_bucket(rel)] - far
    return (b.transpose(3, 0, 1, 2) * LOG2E).astype(F32)


def _place(cols, width, offset):
    z = jnp.zeros((cols.shape[0], width), cols.dtype)
    return z.at[:, offset:offset + cols.shape[1]].set(cols)


def _rot_half(cols):
    half = cols.shape[1] // 2
    return jnp.concatenate([-cols[:, half:], cols[:, :half]], axis=1)


def _prep_in_proj(w_in):
    cq = _place(w_in[:, 0:MLA_Q_LORA], _CQ_PAD, 0)
    ckv = w_in[:, MLA_Q_LORA:MLA_Q_LORA + MLA_KV_LORA]
    kr = w_in[:, MLA_Q_LORA + MLA_KV_LORA:_O_FOX]
    kr_p = _place(kr, LANES, MLA_NOPE)
    kr_rot = _place(_rot_half(kr), LANES, MLA_NOPE)
    fox = w_in[:, _O_FOX:_O_FOX + 3 * GROUP_WIDTH]
    wf = w_in[:, _O_FOX + 3 * GROUP_WIDTH:_O_CONV]
    conv = w_in[:, _O_CONV:_O_DIFF]
    diff = w_in[:, _O_DIFF:_O_DIFF + 3 * GROUP_WIDTH]
    w = jnp.concatenate([cq, ckv, kr_p, kr_rot, fox, conv, diff], axis=1).astype(BF16)
    wf_t = _place(wf, 16, 0).T.astype(BF16)
    return w, wf_t


def _prep_mla(w_uq, w_ukv):
    hd = MLA_NOPE + MLA_ROPE
    plain, rot, k_nope, v = [], [], [], []
    for h in range(4):
        wh = w_uq[:, h * hd:(h + 1) * hd]
        plain.append(_place(wh, LANES, 0))
        rot.append(_place(_rot_half(wh[:, MLA_NOPE:]), LANES, MLA_NOPE))
        kv = w_ukv[:, h * 2 * HEAD_DIM:(h + 1) * 2 * HEAD_DIM]
        k_nope.append(_place(kv[:, :MLA_NOPE], LANES, 0))
        v.append(kv[:, MLA_NOPE:])
    wuq = jnp.concatenate(plain + rot, axis=1)
    wuq = jnp.concatenate([wuq, jnp.zeros((_CQ_PAD - MLA_Q_LORA, wuq.shape[1]), wuq.dtype)], axis=0)
    wukv = jnp.concatenate(k_nope + v, axis=1)
    return wuq.astype(BF16), wukv.astype(BF16)


def _rope_tables(positions):
    half = MLA_ROPE // 2
    inv_freq = ROPE_THETA ** (-jnp.arange(half, dtype=F32) / half)
    ang = positions.astype(F32).reshape(-1)[:, None] * inv_freq
    cos, sin = jnp.cos(ang), jnp.sin(ang)
    n = ang.shape[0]
    cos_t = jnp.ones((n, LANES), F32).at[:, MLA_NOPE:MLA_NOPE + MLA_ROPE].set(jnp.concatenate([cos, cos], 1))
    sin_t = jnp.zeros((n, LANES), F32).at[:, MLA_NOPE:MLA_NOPE + MLA_ROPE].set(jnp.concatenate([sin, sin], 1))
    return cos_t, sin_t


def _dispatch(expert, n_tok):
    tm = MOE_TILE
    n_assign = n_tok * TOP_K
    e_flat = expert.reshape(n_assign)
    order = jnp.argsort(e_flat)
    e_sorted = e_flat[order]
    counts = jnp.zeros((N_EXPERTS,), jnp.int32).at[e_flat].add(1)
    starts = jnp.cumsum(counts) - counts
    padded = (counts + tm - 1) // tm * tm
    pad_ends = jnp.cumsum(padded)
    pad_starts = pad_ends - padded
    dest_sorted = pad_starts[e_sorted] + (jnp.arange(n_assign, dtype=jnp.int32) - starts[e_sorted])
    n_blocks = (n_assign + N_EXPERTS * (tm - 1) + tm - 1) // tm
    slot_tok = jnp.zeros((n_blocks * tm,), jnp.int32).at[dest_sorted].set((order // TOP_K).astype(jnp.int32))
    block_expert = jnp.minimum(
        jnp.searchsorted(pad_ends, jnp.arange(n_blocks, dtype=jnp.int32) * tm, side="right"),
        N_EXPERTS - 1).astype(jnp.int32)
    n_active = (pad_ends[-1] // tm).astype(jnp.int32).reshape(1)
    dest = jnp.zeros((n_assign,), jnp.int32).at[order].set(dest_sorted.astype(jnp.int32)).reshape(n_tok, TOP_K)
    return slot_tok, block_expert, n_active, dest


def kernel(x, c, positions, t5_table, ada_w, ada_b, norm_mix_g, norm_ffn_g, w_in, mla_q_norm_g, mla_w_uq, mla_kv_norm_g, mla_w_ukv, fox_forget_b, conv_w, diff_lambda, diff_subln_g, w_out, router_group_w, router_group_b, router_expert_w, router_expert_b, expert_w_gate, expert_w_up, expert_w_down, final_norm_g):
    bsz, seq, d = x.shape
    n = bsz * seq
    tile = min(ATT_TILE, seq)
    nq = seq // tile
    tc = min(CMB_TILE, seq)

    mods = _ada_call(c, ada_w, ada_b).reshape(DEPTH, bsz, ADA_CHUNKS, D_MODEL)
    cos_t, sin_t = _rope_tables(positions)
    bias_tiles = _t5_tiles(t5_table, tile)
    xf = x.reshape(n, d)

    for layer in range(DEPTH):
        mod = mods[layer]
        w, wf_t = _prep_in_proj(w_in[layer])
        wuq, wukv = _prep_mla(mla_w_uq[layer], mla_w_ukv[layer])
        fb = _place(fox_forget_b[layer][None, :], 16, 0).T.astype(F32)
        qg = _place(mla_q_norm_g[layer][None, :], _CQ_PAD, 0)
        kvg = mla_kv_norm_g[layer][None, :]
        cw = _place(conv_w[layer].T, 8, 0).T
        (qa, ka, va, qb, kb, vb, nf, oc, qd, kd, vd) = _proj_call(
            xf, mod, norm_mix_g[layer][None, :], w, wf_t, fb, wuq, wukv, qg, kvg, cos_t, sin_t, cw, bsz, seq)

        oa = _attn_call("A", qa, ka, va, (), bsz, seq)
        nf_t = nf.reshape(bsz, 8, nq, tile).transpose(0, 2, 1, 3)
        ob = _attn_call("B", qb, kb, vb, (nf_t,), bsz, seq)
        lam_init = 0.8 - 0.6 * math.exp(-0.3 * layer)
        lp = diff_lambda[layer].astype(F32)
        lam = jnp.exp(jnp.sum(lp[0] * lp[1])) - jnp.exp(jnp.sum(lp[2] * lp[3])) + lam_init
        lam_row = jnp.full((1, LANES), lam, F32)
        sg = (jnp.concatenate([diff_subln_g[layer]] * 2) * (1.0 - lam_init))[None, :].astype(F32)
        od = _attn_call("D", qd, kd, vd, (bias_tiles, lam_row, sg), bsz, seq)

        wr = jnp.concatenate([router_group_w[layer], router_expert_w[layer]], axis=1)
        wr_t = _place(wr, 80, 0).T
        wr_hi = wr_t.astype(BF16)
        wr_lo = (wr_t - wr_hi.astype(F32)).astype(BF16)
        rb = _place(jnp.concatenate([router_group_b[layer], router_expert_b[layer]])[None, :], 80, 0).T
        x1, h2, route = _out_router_call(
            oa, ob, oc, od, w_out[layer].astype(BF16), xf, mod, norm_ffn_g[layer][None, :],
            wr_hi, wr_lo, rb.astype(F32), bsz, seq)

        route_t = route.T
        expert = route_t[:, 2:4].astype(jnp.int32)
        slot_tok, block_expert, n_active, dest = _dispatch(expert, n)
        y_slots = _expert_call(block_expert, n_active, slot_tok, h2,
                               expert_w_gate[layer], expert_w_up[layer], expert_w_down[layer])
        dest_t = dest.reshape(n // tc, tc, TOP_K).transpose(0, 2, 1).reshape(n // tc, 1, TOP_K * tc)
        xf = _combine_call(dest_t, y_slots, x1, route_t, mod, final_norm_g[None, :], bsz, seq,
                           final=(layer == DEPTH - 1))
    return xf.reshape(bsz, seq, d)
```

```python
import functools
import math

import jax
import jax.numpy as jnp
from jax import lax
from jax.experimental import pallas as pl
from jax.experimental.pallas import tpu as pltpu

F32 = jnp.float32
BF16 = jnp.bfloat16

D_MODEL = 1024
DEPTH = 4
CHUNK = 64
EPS = 1e-6
HEAD_DIM = 64
GROUP_WIDTH = 256
MLA_NOPE = 64
MLA_ROPE = 32
MLA_Q_LORA = 192
MLA_KV_LORA = 128
ROPE_THETA = 10000.0
DIFF_HALF = 32
T5_BUCKETS = 32
T5_MAX_DIST = 128
N_GROUPS = 8
EXPERTS_PER_GROUP = 8
N_EXPERTS = 64
TOP_K = 2
D_EXPERT = 512
ADA_CHUNKS = 6

LANES = 128
LOG2E = 1.4426950408889634
NEG = -1e30

_O_MLA = 0
_O_FOX = MLA_Q_LORA + MLA_KV_LORA + MLA_ROPE
_O_CONV = _O_FOX + 3 * GROUP_WIDTH + 4
_O_DIFF = _O_CONV + 3 * GROUP_WIDTH

_CQ_PAD = 256
_W_A = _CQ_PAD + MLA_KV_LORA + 2 * LANES
_W_B = 3 * GROUP_WIDTH
_W_C = 3 * GROUP_WIDTH
_W_D = 3 * GROUP_WIDTH
_P_TOTAL = _W_A + _W_B + _W_C + _W_D

ROW_TILE = 512
ATT_Q = 256
ATT_K = 256
MOE_TILE = 256
CMB_TILE = 256
VMEM_LIMIT = 56 * 1024 * 1024


def _cparams(sem):
    return pltpu.CompilerParams(dimension_semantics=sem, vmem_limit_bytes=VMEM_LIMIT)


def _split_bf16(a):
    hi = a.astype(BF16)
    lo = (a - hi.astype(F32)).astype(BF16)
    return hi, lo


def _dot(a, b):
    return jnp.dot(a, b, preferred_element_type=F32)


def _dot_nt(a, b):
    return lax.dot_general(a, b, (((1,), (1,)), ((), ())), preferred_element_type=F32)


def _sigmoid(z):
    return 1.0 / (1.0 + jnp.exp(-z))


def _ada_kernel(c_ref, w_ref, b_ref, o_ref):
    c = c_ref[...]
    cond = c * _sigmoid(c)
    c_hi, c_lo = _split_bf16(cond)
    w_hi, w_lo = _split_bf16(w_ref[0])
    o_ref[0] = _dot(c_hi, w_hi) + _dot(c_hi, w_lo) + _dot(c_lo, w_hi) + b_ref[0]


def _ada_call(c, ada_w, ada_b):
    bsz = c.shape[0]
    n_col = ADA_CHUNKS * D_MODEL // D_MODEL
    return pl.pallas_call(
        _ada_kernel,
        grid=(DEPTH, n_col),
        in_specs=[
            pl.BlockSpec((bsz, D_MODEL), lambda l, j: (0, 0)),
            pl.BlockSpec((1, D_MODEL, D_MODEL), lambda l, j: (l, 0, j)),
            pl.BlockSpec((1, 1, D_MODEL), lambda l, j: (l, 0, j)),
        ],
        out_specs=pl.BlockSpec((1, bsz, D_MODEL), lambda l, j: (l, 0, j)),
        out_shape=jax.ShapeDtypeStruct((DEPTH, bsz, ADA_CHUNKS * D_MODEL), F32),
        compiler_params=_cparams(("arbitrary", "arbitrary")),
        name="ada_mod",
    )(c, ada_w, ada_b.reshape(DEPTH, 1, ADA_CHUNKS * D_MODEL))


def _values_with_ones(v):
    lane = lax.broadcasted_iota(jnp.int32, (1, LANES), 1)
    low_half = lane < HEAD_DIM
    slabs = []
    for head in range(4):
        pair = v[:, (head // 2) * LANES:(head // 2 + 1) * LANES]
        keep = low_half if head % 2 == 0 else jnp.logical_not(low_half)
        slabs.append(jnp.where(keep, pair, 1.0))
    return jnp.concatenate(slabs, axis=-1).astype(BF16)


def _proj_kernel(x_ref, mod_ref, g_ref, w_ref, wf_ref, fb_ref, wuq_ref, wukv_ref, qg_ref, kvg_ref,
                 cos_ref, sin_ref, cw_ref,
                 qa_ref, ka_ref, va_ref, qb_ref, kb_ref, vb_ref, nf_ref, oc_ref, qd_ref, kd_ref, vd_ref,
                 zc_ref, fc_ref, *, scale_a, scale_b, scale_d):
    t = pl.program_id(1)
    rows = x_ref.shape[0]

    @pl.when(t == 0)
    def _():
        zc_ref[...] = jnp.zeros_like(zc_ref)
        fc_ref[...] = jnp.zeros_like(fc_ref)

    x = x_ref[...]
    mod = mod_ref[0]
    hn = x * lax.rsqrt(jnp.mean(x * x, axis=-1, keepdims=True) + EPS) * g_ref[...]
    h = hn * (1.0 + mod[1:2]) + mod[0:1]
    hb = h.astype(BF16)

    z = _dot_nt(wf_ref[...], hb) + fb_ref[...]
    log_f = jnp.minimum(z, 0.0) - jnp.log(1.0 + jnp.exp(-jnp.abs(z)))
    r_i = lax.broadcasted_iota(jnp.int32, (rows, rows), 0)
    c_i = lax.broadcasted_iota(jnp.int32, (rows, rows), 1)
    tri = jnp.where(r_i <= c_i, 1.0, 0.0).astype(BF16)
    f_hi = log_f.astype(BF16)
    rem = log_f - f_hi.astype(F32)
    f_mid = rem.astype(BF16)
    f_lo = (rem - f_mid.astype(F32)).astype(BF16)
    cum = _dot(f_hi, tri) + _dot(f_mid, tri) + _dot(f_lo, tri) + fc_ref[:, 0:1]
    fc_ref[...] = jnp.broadcast_to(cum[:, rows - 1:rows], fc_ref.shape)
    nf_ref[0] = cum[0:8] * (-LOG2E)

    pa = _dot(hb, w_ref[:, 0:_W_A])
    cq = pa[:, 0:_CQ_PAD]
    ckv = pa[:, _CQ_PAD:_CQ_PAD + MLA_KV_LORA]
    kr = pa[:, _CQ_PAD + MLA_KV_LORA:_CQ_PAD + MLA_KV_LORA + LANES]
    krr = pa[:, _CQ_PAD + MLA_KV_LORA + LANES:_W_A]
    cqn = cq * lax.rsqrt(jnp.sum(cq * cq, axis=-1, keepdims=True) * (1.0 / MLA_Q_LORA) + EPS) * qg_ref[...]
    ckvn = ckv * lax.rsqrt(jnp.mean(ckv * ckv, axis=-1, keepdims=True) + EPS) * kvg_ref[...]
    q2 = _dot(cqn.astype(BF16), wuq_ref[...])
    kv2 = _dot(ckvn.astype(BF16), wukv_ref[...])
    cos = cos_ref[...]
    sin = sin_ref[...]
    cos4 = jnp.concatenate([cos] * 4, axis=-1)
    sin4 = jnp.concatenate([sin] * 4, axis=-1)
    qa = (q2[:, 0:512] * cos4 + q2[:, 512:1024] * sin4) * (scale_a * LOG2E)
    k_rope = kr * cos + krr * sin
    ka = kv2[:, 0:512] + jnp.concatenate([k_rope] * 4, axis=-1)
    qa_ref[...] = qa.astype(BF16)
    ka_ref[...] = ka.astype(BF16)
    va_ref[...] = _values_with_ones(kv2[:, 512:768])

    pb = _dot(hb, w_ref[:, _W_A:_W_A + _W_B])
    qb_ref[...] = (pb[:, 0:256] * (scale_b * LOG2E)).astype(BF16)
    kb_ref[...] = pb[:, 256:512].astype(BF16)
    vb_ref[...] = _values_with_ones(pb[:, 512:768])

    pc = _dot(hb, w_ref[:, _W_A + _W_B:_W_A + _W_B + _W_C])
    zz = pc[:, 256:512] * pc[:, 512:768]
    ext = jnp.concatenate([zc_ref[...], zz], axis=0)
    cw = cw_ref[...]
    conv = zz * cw[2:3] + ext[7:rows + 7] * cw[1:2] + ext[6:rows + 6] * cw[0:1]
    zc_ref[...] = zz[rows - 8:rows]
    oc_ref[...] = (pc[:, 0:256] * conv).astype(BF16)

    pd = _dot(hb, w_ref[:, _W_A + _W_B + _W_C:_P_TOTAL])
    qd_ref[...] = (pd[:, 0:256] * (scale_d * LOG2E)).astype(BF16)
    kd_ref[...] = pd[:, 256:512].astype(BF16)
    vd_ref[...] = _values_with_ones(pd[:, 512:768])


def _proj_call(x, mod, g, w, wf, fb, wuq, wukv, qg, kvg, cos_t, sin_t, cw, bsz, seq):
    n = bsz * seq
    rt = min(ROW_TILE, seq)
    nt = seq // rt
    row = lambda width: pl.BlockSpec((rt, width), lambda b, t: (b * nt + t, 0))
    full = lambda a: pl.BlockSpec(a.shape, lambda b, t: (0,) * a.ndim)
    bf = lambda width: jax.ShapeDtypeStruct((n, width), BF16)
    kern = functools.partial(
        _proj_kernel,
        scale_a=(MLA_NOPE + MLA_ROPE) ** -0.5, scale_b=HEAD_DIM ** -0.5, scale_d=DIFF_HALF ** -0.5)
    return pl.pallas_call(
        kern,
        grid=(bsz, nt),
        in_specs=[
            row(D_MODEL),
            pl.BlockSpec((1, ADA_CHUNKS, D_MODEL), lambda b, t: (b, 0, 0)),
            full(g), full(w), full(wf), full(fb), full(wuq), full(wukv), full(qg), full(kvg),
            row(LANES), row(LANES), full(cw),
        ],
        out_specs=[
            row(512), row(512), row(512), row(256), row(256), row(512),
            pl.BlockSpec((1, 8, rt), lambda b, t: (b, 0, t)),
            row(256), row(256), row(256), row(512),
        ],
        out_shape=[
            bf(512), bf(512), bf(512), bf(256), bf(256), bf(512),
            jax.ShapeDtypeStruct((bsz, 8, seq), F32),
            bf(256), bf(256), bf(256), bf(512),
        ],
        scratch_shapes=[pltpu.VMEM((8, GROUP_WIDTH), F32), pltpu.VMEM((16, LANES), F32)],
        compiler_params=_cparams(("arbitrary", "arbitrary")),
        name="norm_in_proj",
    )(x, mod, g, w, wf, fb, wuq, wukv, qg, kvg, cos_t, sin_t, cw)


def _tile_update(q, k_t, v_t, m_ref, acc_ref, bias=None, mask=None):
    s = _dot_nt(q, k_t)
    if bias is not None:
        s = s + bias
    if mask is not None:
        s = jnp.where(mask, s, NEG)
    m_prev = m_ref[...]
    m_new = jnp.maximum(m_prev, jnp.max(s, axis=-1, keepdims=True))
    alpha = jnp.exp2(m_prev - m_new)
    p = jnp.exp2(s - jnp.concatenate([m_new] * (s.shape[1] // LANES), axis=-1))
    acc_ref[...] = alpha * acc_ref[...] + _dot(p.astype(BF16), v_t)
    m_ref[...] = m_new


def _attn_kernel(*refs, kind, tq, tk):
    n_maps = 8 if kind == "D" else 4
    n_in = {"A": 3, "B": 4, "D": 6}[kind]
    q_ref, k_ref, v_ref = refs[0:3]
    o_ref = refs[n_in]
    q_scr = refs[n_in + 1]
    m_refs = refs[n_in + 2:n_in + 2 + n_maps]
    acc_refs = refs[n_in + 2 + n_maps:n_in + 2 + 2 * n_maps]
    if kind == "B":
        nf_ref = refs[3]
    elif kind == "D":
        bias_ref, lam_ref, sg_ref = refs[3:6]
    i = pl.program_id(1)
    ratio = tk // tq
    jd = i // ratio
    par = i % ratio
    off = par * tq
    lane = lax.broadcasted_iota(jnp.int32, (1, LANES), 1)
    low_half = lane < HEAD_DIM
    r_i = lax.broadcasted_iota(jnp.int32, (tq, tk), 0) + off
    c_i = lax.broadcasted_iota(jnp.int32, (tq, tk), 1)
    if kind == "B":
        diag_mask = c_i <= r_i
    else:
        shift = CHUNK.bit_length() - 1
        diag_mask = jnp.right_shift(c_i, shift) <= jnp.right_shift(r_i, shift)

    maps = []
    for head in range(4):
        slab = head // 2
        if kind == "A":
            q_scr[head] = q_ref[:, head * LANES:(head + 1) * LANES]
            maps.append((head, head, head))
        elif kind == "B":
            q = q_ref[:, slab * LANES:(slab + 1) * LANES]
            sel = low_half if head % 2 == 0 else jnp.logical_not(low_half)
            q_scr[head] = jnp.where(sel, q, jnp.zeros_like(q))
            maps.append((head, slab, head))
        else:
            q = q_ref[:, slab * LANES:(slab + 1) * LANES]
            for mp in range(2):
                lo = (2 * (head % 2) + mp) * DIFF_HALF
                sel = (lane >= lo) & (lane < lo + DIFF_HALF)
                q_scr[2 * head + mp] = jnp.where(sel, q, jnp.zeros_like(q))
                maps.append((2 * head + mp, slab, head))
    for idx in range(n_maps):
        m_refs[idx][...] = jnp.full(m_refs[idx].shape, NEG, F32)
        acc_refs[idx][...] = jnp.zeros(acc_refs[idx].shape, F32)

    def step(j, mode):
        start = pl.multiple_of(j * tk, tk)
        for idx, k_slab, head in maps:
            k_t = k_ref[pl.ds(start, tk), k_slab * LANES:(k_slab + 1) * LANES]
            v_t = v_ref[pl.ds(start, tk), head * LANES:(head + 1) * LANES]
            bias = None
            if kind == "B":
                bias = nf_ref[0, j][head:head + 1, :]
            elif kind == "D" and mode != "far":
                bias = bias_ref[head, par, 0 if mode == "diag" else 1]
            _tile_update(q_scr[idx], k_t, v_t, m_refs[idx], acc_refs[idx], bias=bias,
                         mask=diag_mask if mode == "diag" else None)

    def far_pair(jj, carry):
        step(2 * jj, "far")
        step(2 * jj + 1, "far")
        return carry

    n_far = jnp.maximum(jd - 1, 0)
    lax.fori_loop(0, n_far // 2, far_pair, 0)

    @pl.when(n_far % 2 == 1)
    def _():
        step(n_far - 1, "far")

    @pl.when(jd >= 1)
    def _():
        step(jd - 1, "prev")
        step(jd, "diag")

    @pl.when(jd == 0)
    def _():
        step(jd, "diag")

    def normalized(idx):
        acc = acc_refs[idx][...]
        return acc / pltpu.roll(acc, HEAD_DIM, axis=1)

    for slab in range(2):
        outs = []
        for sub in range(2):
            head = 2 * slab + sub
            if kind == "D":
                outs.append(normalized(2 * head) - lam_ref[...] * normalized(2 * head + 1))
            else:
                outs.append(normalized(head))
        o_slab = jnp.where(low_half, outs[0], outs[1])
        if kind == "D":
            sq = o_slab * o_slab
            s_lo = jnp.sum(jnp.where(low_half, sq, 0.0), axis=-1, keepdims=True)
            s_hi = jnp.sum(jnp.where(low_half, 0.0, sq), axis=-1, keepdims=True)
            ms = jnp.where(low_half, s_lo, s_hi) * (1.0 / HEAD_DIM)
            o_slab = o_slab * lax.rsqrt(ms + EPS) * sg_ref[...]
        o_ref[:, slab * LANES:(slab + 1) * LANES] = o_slab.astype(BF16)


def _attn_call(kind, q, k, v, extra, bsz, seq):
    n = bsz * seq
    tq = min(ATT_Q, seq)
    tk = min(ATT_K, seq)
    nq = seq // tq
    wq = q.shape[1]
    in_specs = [
        pl.BlockSpec((tq, wq), lambda b, i: (b * nq + i, 0)),
        pl.BlockSpec((seq, k.shape[1]), lambda b, i: (b, 0)),
        pl.BlockSpec((seq, v.shape[1]), lambda b, i: (b, 0)),
    ]
    if kind == "B":
        in_specs.append(pl.BlockSpec((1, seq // tk, 8, tk), lambda b, i: (b, 0, 0, 0)))
    elif kind == "D":
        bias, lam, sg = extra
        in_specs += [
            pl.BlockSpec(bias.shape, lambda b, i: (0,) * bias.ndim),
            pl.BlockSpec(lam.shape, lambda b, i: (0, 0)),
            pl.BlockSpec(sg.shape, lambda b, i: (0, 0)),
        ]
    n_maps = 8 if kind == "D" else 4
    return pl.pallas_call(
        functools.partial(_attn_kernel, kind=kind, tq=tq, tk=tk),
        grid=(bsz, nq),
        in_specs=in_specs,
        out_specs=pl.BlockSpec((tq, GROUP_WIDTH), lambda b, i: (b * nq + i, 0)),
        out_shape=jax.ShapeDtypeStruct((n, GROUP_WIDTH), BF16),
        scratch_shapes=(
            [pltpu.VMEM((n_maps, tq, LANES), BF16)]
            + [pltpu.VMEM((tq, LANES), F32) for _ in range(2 * n_maps)]
        ),
        compiler_params=_cparams(("arbitrary", "arbitrary")),
        name="attn_" + kind,
    )(q, k, v, *extra)


def _out_router_kernel(oa_ref, ob_ref, oc_ref, od_ref, wo_ref, x_ref, mod_ref, g_ref, wr_hi_ref, wr_lo_ref,
                       rb_ref, xo_ref, h2_ref, route_ref):
    gw = GROUP_WIDTH
    mix = (_dot(oa_ref[...], wo_ref[0:gw]) + _dot(ob_ref[...], wo_ref[gw:2 * gw])
           + _dot(oc_ref[...], wo_ref[2 * gw:3 * gw]) + _dot(od_ref[...], wo_ref[3 * gw:4 * gw]))
    mod = mod_ref[0]
    x1 = x_ref[...] + mod[2:3] * mix
    xo_ref[...] = x1
    hn = x1 * lax.rsqrt(jnp.mean(x1 * x1, axis=-1, keepdims=True) + EPS) * g_ref[...]
    h2 = hn * (1.0 + mod[4:5]) + mod[3:4]
    h2_ref[...] = h2

    a_hi, a_lo = _split_bf16(h2)
    w_hi = wr_hi_ref[...]
    lg = _dot_nt(w_hi, a_hi) + _dot_nt(w_hi, a_lo) + _dot_nt(wr_lo_ref[...], a_hi) + rb_ref[...]
    gl = lg[0:N_GROUPS]
    rows = gl.shape[1]
    iota8 = lax.broadcasted_iota(jnp.int32, (N_GROUPS, rows), 0)
    gmax = jnp.max(gl, axis=0, keepdims=True)
    grp = jnp.min(jnp.where(gl == gmax, iota8, N_GROUPS), axis=0, keepdims=True)
    p_grp = 1.0 / jnp.sum(jnp.exp(gl - gmax), axis=0, keepdims=True)
    esel = jnp.zeros((EXPERTS_PER_GROUP, rows), F32)
    for gi in range(N_GROUPS):
        lo = N_GROUPS + gi * EXPERTS_PER_GROUP
        esel = jnp.where(grp == gi, lg[lo:lo + EXPERTS_PER_GROUP], esel)
    ee = jnp.exp(esel - jnp.max(esel, axis=0, keepdims=True))
    ps = ee / jnp.sum(ee, axis=0, keepdims=True)
    p1 = jnp.max(ps, axis=0, keepdims=True)
    i1 = jnp.min(jnp.where(ps == p1, iota8, EXPERTS_PER_GROUP), axis=0, keepdims=True)
    ps2 = jnp.where(iota8 == i1, -1.0, ps)
    p2 = jnp.max(ps2, axis=0, keepdims=True)
    i2 = jnp.min(jnp.where(ps2 == p2, iota8, EXPERTS_PER_GROUP), axis=0, keepdims=True)
    den = p1 + p2
    g1 = p_grp * p1 / den
    g2 = p_grp * p2 / den
    e1 = (grp * EXPERTS_PER_GROUP + i1).astype(F32)
    e2 = (grp * EXPERTS_PER_GROUP + i2).astype(F32)
    zero = jnp.zeros_like(g1)
    route_ref[...] = jnp.concatenate([g1, g2, e1, e2, zero, zero, zero, zero], axis=0)


def _out_router_call(oa, ob, oc, od, wo, x, mod, g, wr_hi, wr_lo, rb, bsz, seq):
    n = bsz * seq
    rt = min(ROW_TILE, seq)
    nt = seq // rt
    row = lambda width: pl.BlockSpec((rt, width), lambda t: (t, 0))
    full = lambda a: pl.BlockSpec(a.shape, lambda t: (0,) * a.ndim)
    return pl.pallas_call(
        _out_router_kernel,
        grid=(n // rt,),
        in_specs=[
            row(GROUP_WIDTH), row(GROUP_WIDTH), row(GROUP_WIDTH), row(GROUP_WIDTH), full(wo),
            row(D_MODEL),
            pl.BlockSpec((1, ADA_CHUNKS, D_MODEL), lambda t: (t // nt, 0, 0)),
            full(g), full(wr_hi), full(wr_lo), full(rb),
        ],
        out_specs=[row(D_MODEL), row(D_MODEL), pl.BlockSpec((8, rt), lambda t: (0, t))],
        out_shape=[
            jax.ShapeDtypeStruct((n, D_MODEL), F32),
            jax.ShapeDtypeStruct((n, D_MODEL), F32),
            jax.ShapeDtypeStruct((8, n), F32),
        ],
        compiler_params=_cparams(("arbitrary",)),
        name="out_proj_router",
    )(oa, ob, oc, od, wo, x, mod, g, wr_hi, wr_lo, rb)


def _row_gather(idx_of, src_hbm, dst, sem, n_rows):
    def issue(r, carry):
        pltpu.make_async_copy(src_hbm.at[pl.ds(idx_of(r), 1)], dst.at[pl.ds(r, 1)], sem).start()
        return carry
    lax.fori_loop(0, n_rows, issue, 0)


def _scatter_kernel(dest_ref, h_ref, init_hbm, xs_hbm, sem):
    del init_hbm
    rows = h_ref.shape[0]

    def issue(r, carry):
        for k in range(TOP_K):
            pltpu.make_async_copy(
                h_ref.at[pl.ds(r, 1)], xs_hbm.at[pl.ds(dest_ref[0, 0, k * rows + r], 1)], sem).start()
        return carry
    lax.fori_loop(0, rows, issue, 0)
    for k in range(TOP_K):
        pltpu.make_async_copy(h_ref, xs_hbm.at[pl.ds(0, rows)], sem).wait()


def _scatter_call(dest_t, h2, n_slots):
    n = h2.shape[0]
    td = dest_t.shape[2] // TOP_K
    init = jnp.zeros((n_slots, D_MODEL), F32)
    return pl.pallas_call(
        _scatter_kernel,
        grid=(n // td,),
        in_specs=[
            pl.BlockSpec((1, 1, TOP_K * td), lambda t: (t, 0, 0), memory_space=pltpu.SMEM),
            pl.BlockSpec((td, D_MODEL), lambda t: (t, 0)),
            pl.BlockSpec(memory_space=pl.ANY),
        ],
        out_specs=pl.BlockSpec(memory_space=pl.ANY),
        out_shape=jax.ShapeDtypeStruct((n_slots, D_MODEL), F32),
        scratch_shapes=[pltpu.SemaphoreType.DMA],
        input_output_aliases={2: 0},
        compiler_params=_cparams(("arbitrary",)),
        name="moe_dispatch",
    )(dest_t, h2, init)


def _expert_kernel(be_ref, nact_ref, x_ref, wg_ref, wu_ref, wd_ref, y_ref, wgb, wub, wdb):
    i = pl.program_id(0)

    @pl.when(i < nact_ref[0])
    def _():
        prev = be_ref[jnp.maximum(i - 1, 0)]

        @pl.when((i == 0) | (be_ref[i] != prev))
        def _():
            wgb[...] = wg_ref[0, 0].astype(BF16)
            wub[...] = wu_ref[0, 0].astype(BF16)
            wdb[...] = wd_ref[0, 0].astype(BF16)

        xb = x_ref[...].astype(BF16)
        gate = _dot(xb, wgb[...])
        up = _dot(xb, wub[...])
        hid = gate * _sigmoid(gate) * up
        y_ref[...] = _dot(hid.astype(BF16), wdb[...])

    @pl.when(i >= nact_ref[0])
    def _():
        y_ref[...] = jnp.zeros_like(y_ref)


def _expert_call(block_expert, n_active, x_slots, w_gate, w_up, w_down, layer):
    n_blocks = block_expert.shape[0]
    tm = MOE_TILE
    w_idx = lambda i, be, na: (layer, be[i], 0, 0)
    grid_spec = pltpu.PrefetchScalarGridSpec(
        num_scalar_prefetch=2,
        grid=(n_blocks,),
        in_specs=[
            pl.BlockSpec((tm, D_MODEL), lambda i, be, na: (jnp.minimum(i, na[0] - 1), 0)),
            pl.BlockSpec((1, 1, D_MODEL, D_EXPERT), w_idx),
            pl.BlockSpec((1, 1, D_MODEL, D_EXPERT), w_idx),
            pl.BlockSpec((1, 1, D_EXPERT, D_MODEL), w_idx),
        ],
        out_specs=pl.BlockSpec((tm, D_MODEL), lambda i, be, na: (i, 0)),
        scratch_shapes=[
            pltpu.VMEM((D_MODEL, D_EXPERT), BF16),
            pltpu.VMEM((D_MODEL, D_EXPERT), BF16),
            pltpu.VMEM((D_EXPERT, D_MODEL), BF16),
        ],
    )
    return pl.pallas_call(
        _expert_kernel,
        grid_spec=grid_spec,
        out_shape=jax.ShapeDtypeStruct((n_blocks * tm, D_MODEL), F32),
        compiler_params=_cparams(("arbitrary",)),
        name="expert_mlp",
    )(block_expert, n_active, x_slots, w_gate, w_up, w_down)


def _combine_kernel(dest_ref, y_hbm, x_ref, gate_ref, mod_ref, fg_ref, o_ref, ybuf, sem, *, final):
    rows = x_ref.shape[0]
    for k in range(TOP_K):
        _row_gather(lambda r, k=k: dest_ref[0, 0, k * rows + r], y_hbm, ybuf.at[k], sem.at[k], rows)
    for k in range(TOP_K):
        pltpu.make_async_copy(y_hbm.at[pl.ds(0, rows)], ybuf.at[k], sem.at[k]).wait()
    gt = gate_ref[...]
    y = gt[:, 0:1] * ybuf[0] + gt[:, 1:2] * ybuf[1]
    x2 = x_ref[...] + mod_ref[0][5:6] * y
    if final:
        x2 = x2 * lax.rsqrt(jnp.mean(x2 * x2, axis=-1, keepdims=True) + EPS) * fg_ref[...]
    o_ref[...] = x2


def _combine_call(dest, y_slots, x, gates, mod, fg, bsz, seq, final):
    n = bsz * seq
    tc = min(CMB_TILE, seq)
    nt = seq // tc
    return pl.pallas_call(
        functools.partial(_combine_kernel, final=final),
        grid=(n // tc,),
        in_specs=[
            pl.BlockSpec((1, 1, TOP_K * tc), lambda t: (t, 0, 0), memory_space=pltpu.SMEM),
            pl.BlockSpec(memory_space=pl.ANY),
            pl.BlockSpec((tc, D_MODEL), lambda t: (t, 0)),
            pl.BlockSpec((tc, 8), lambda t: (t, 0)),
            pl.BlockSpec((1, ADA_CHUNKS, D_MODEL), lambda t: (t // nt, 0, 0)),
            pl.BlockSpec((1, D_MODEL), lambda t: (0, 0)),
        ],
        out_specs=pl.BlockSpec((tc, D_MODEL), lambda t: (t, 0)),
        out_shape=jax.ShapeDtypeStruct((n, D_MODEL), F32),
        scratch_shapes=[pltpu.VMEM((TOP_K, tc, D_MODEL), F32), pltpu.SemaphoreType.DMA((TOP_K,))],
        compiler_params=_cparams(("arbitrary",)),
        name="moe_combine",
    )(dest, y_slots, x, gates, mod, fg)


def _t5_bucket(rel):
    nb = T5_BUCKETS // 2
    max_exact = nb // 2
    bucket = jnp.where(rel > 0, nb, 0)
    n = jnp.abs(rel)
    large = max_exact + (jnp.log(jnp.maximum(n, 1).astype(F32) / max_exact)
                         / math.log(T5_MAX_DIST / max_exact) * (nb - max_exact)).astype(jnp.int32)
    large = jnp.minimum(large, nb - 1)
    return bucket + jnp.where(n < max_exact, n, large)


def _t5_tiles(t5_table, tq, tk):
    assert tk >= T5_MAX_DIST and tk % tq == 0
    par = jnp.arange(tk // tq, dtype=jnp.int32)[:, None, None, None]
    back = jnp.arange(2, dtype=jnp.int32)[None, :, None, None]
    r = jnp.arange(tq, dtype=jnp.int32)[None, None, :, None]
    c = jnp.arange(tk, dtype=jnp.int32)[None, None, None, :]
    rel = c - back * tk - (r + par * tq)
    far = t5_table[_t5_bucket(jnp.array(-4 * T5_MAX_DIST, jnp.int32))]
    b = t5_table[_t5_bucket(rel)] - far
    return (b.transpose(4, 0, 1, 2, 3) * LOG2E).astype(F32)


def _place(cols, width, offset):
    z = jnp.zeros((cols.shape[0], width), cols.dtype)
    return z.at[:, offset:offset + cols.shape[1]].set(cols)


def _rot_half(cols):
    half = cols.shape[1] // 2
    return jnp.concatenate([-cols[:, half:], cols[:, :half]], axis=1)


def _prep_in_proj(w_in):
    cq = _place(w_in[:, 0:MLA_Q_LORA], _CQ_PAD, 0)
    ckv = w_in[:, MLA_Q_LORA:MLA_Q_LORA + MLA_KV_LORA]
    kr = w_in[:, MLA_Q_LORA + MLA_KV_LORA:_O_FOX]
    kr_p = _place(kr, LANES, MLA_NOPE)
    kr_rot = _place(_rot_half(kr), LANES, MLA_NOPE)
    fox = w_in[:, _O_FOX:_O_FOX + 3 * GROUP_WIDTH]
    wf = w_in[:, _O_FOX + 3 * GROUP_WIDTH:_O_CONV]
    conv = w_in[:, _O_CONV:_O_DIFF]
    diff = w_in[:, _O_DIFF:_O_DIFF + 3 * GROUP_WIDTH]
    w = jnp.concatenate([cq, ckv, kr_p, kr_rot, fox, conv, diff], axis=1).astype(BF16)
    wf_t = _place(wf, 16, 0).T.astype(BF16)
    return w, wf_t


def _prep_mla(w_uq, w_ukv):
    hd = MLA_NOPE + MLA_ROPE
    plain, rot, k_nope, v = [], [], [], []
    for h in range(4):
        wh = w_uq[:, h * hd:(h + 1) * hd]
        plain.append(_place(wh, LANES, 0))
        rot.append(_place(_rot_half(wh[:, MLA_NOPE:]), LANES, MLA_NOPE))
        kv = w_ukv[:, h * 2 * HEAD_DIM:(h + 1) * 2 * HEAD_DIM]
        k_nope.append(_place(kv[:, :MLA_NOPE], LANES, 0))
        v.append(kv[:, MLA_NOPE:])
    wuq = jnp.concatenate(plain + rot, axis=1)
    wuq = jnp.concatenate([wuq, jnp.zeros((_CQ_PAD - MLA_Q_LORA, wuq.shape[1]), wuq.dtype)], axis=0)
    wukv = jnp.concatenate(k_nope + v, axis=1)
    return wuq.astype(BF16), wukv.astype(BF16)


def _rope_tables(positions):
    half = MLA_ROPE // 2
    inv_freq = ROPE_THETA ** (-jnp.arange(half, dtype=F32) / half)
    ang = positions.astype(F32).reshape(-1)[:, None] * inv_freq
    cos, sin = jnp.cos(ang), jnp.sin(ang)
    n = ang.shape[0]
    cos_t = jnp.ones((n, LANES), F32).at[:, MLA_NOPE:MLA_NOPE + MLA_ROPE].set(jnp.concatenate([cos, cos], 1))
    sin_t = jnp.zeros((n, LANES), F32).at[:, MLA_NOPE:MLA_NOPE + MLA_ROPE].set(jnp.concatenate([sin, sin], 1))
    return cos_t, sin_t


def _dispatch(expert, n_tok):
    tm = MOE_TILE
    n_assign = n_tok * TOP_K
    chunk = LANES
    e_flat = expert.reshape(n_assign)
    onehot = e_flat[:, None] == jnp.arange(N_EXPERTS, dtype=jnp.int32)[None, :]
    oh = onehot.astype(BF16).reshape(n_assign // chunk, chunk, N_EXPERTS)
    tri = (jnp.arange(chunk)[:, None] >= jnp.arange(chunk)[None, :]).astype(BF16)
    within = jnp.einsum("ij,cjk->cik", tri, oh, preferred_element_type=F32)
    total = within[:, chunk - 1, :]
    before = jnp.cumsum(total, axis=0) - total
    running = (within + before[:, None, :]).reshape(n_assign, N_EXPERTS)
    counts = (before[-1] + total[-1]).astype(jnp.int32)
    padded = (counts + tm - 1) // tm * tm
    pad_ends = jnp.cumsum(padded)
    pad_starts = pad_ends - padded
    slot = jnp.sum(jnp.where(onehot, running - 1.0 + pad_starts.astype(F32)[None, :], 0.0), axis=1)
    dest = slot.astype(jnp.int32).reshape(n_tok, TOP_K)
    n_blocks = (n_assign + N_EXPERTS * (tm - 1) + tm - 1) // tm
    block_start = jnp.arange(n_blocks, dtype=jnp.int32) * tm
    block_expert = jnp.minimum(
        jnp.sum((pad_ends[None, :] <= block_start[:, None]).astype(jnp.int32), axis=1), N_EXPERTS - 1)
    n_active = (pad_ends[-1] // tm).astype(jnp.int32).reshape(1)
    return block_expert, n_active, dest, n_blocks * tm


def kernel(x, c, positions, t5_table, ada_w, ada_b, norm_mix_g, norm_ffn_g, w_in, mla_q_norm_g, mla_w_uq, mla_kv_norm_g, mla_w_ukv, fox_forget_b, conv_w, diff_lambda, diff_subln_g, w_out, router_group_w, router_group_b, router_expert_w, router_expert_b, expert_w_gate, expert_w_up, expert_w_down, final_norm_g):
    bsz, seq, d = x.shape
    n = bsz * seq
    tq = min(ATT_Q, seq)
    tk = min(ATT_K, seq)
    tc = min(CMB_TILE, seq)

    mods = _ada_call(c, ada_w, ada_b).reshape(DEPTH, bsz, ADA_CHUNKS, D_MODEL)
    cos_t, sin_t = _rope_tables(positions)
    bias_tiles = _t5_tiles(t5_table, tq, tk)
    xf = x.reshape(n, d)

    for layer in range(DEPTH):
        mod = mods[layer]
        w, wf_t = _prep_in_proj(w_in[layer])
        wuq, wukv = _prep_mla(mla_w_uq[layer], mla_w_ukv[layer])
        fb = _place(fox_forget_b[layer][None, :], 16, 0).T.astype(F32)
        qg = _place(mla_q_norm_g[layer][None, :], _CQ_PAD, 0)
        kvg = mla_kv_norm_g[layer][None, :]
        cw = _place(conv_w[layer].T, 8, 0).T
        (qa, ka, va, qb, kb, vb, nf, oc, qd, kd, vd) = _proj_call(
            xf, mod, norm_mix_g[layer][None, :], w, wf_t, fb, wuq, wukv, qg, kvg, cos_t, sin_t, cw, bsz, seq)

        oa = _attn_call("A", qa, ka, va, (), bsz, seq)
        nf_t = nf.reshape(bsz, 8, seq // tk, tk).transpose(0, 2, 1, 3)
        ob = _attn_call("B", qb, kb, vb, (nf_t,), bsz, seq)
        lam_init = 0.8 - 0.6 * math.exp(-0.3 * layer)
        lp = diff_lambda[layer].astype(F32)
        lam = jnp.exp(jnp.sum(lp[0] * lp[1])) - jnp.exp(jnp.sum(lp[2] * lp[3])) + lam_init
        lam_row = jnp.full((1, LANES), lam, F32)
        sg = (jnp.concatenate([diff_subln_g[layer]] * 2) * (1.0 - lam_init))[None, :].astype(F32)
        od = _attn_call("D", qd, kd, vd, (bias_tiles, lam_row, sg), bsz, seq)

        wr = jnp.concatenate([router_group_w[layer], router_expert_w[layer]], axis=1)
        wr_t = _place(wr, 80, 0).T
        wr_hi = wr_t.astype(BF16)
        wr_lo = (wr_t - wr_hi.astype(F32)).astype(BF16)
        rb = _place(jnp.concatenate([router_group_b[layer], router_expert_b[layer]])[None, :], 80, 0).T
        x1, h2, route = _out_router_call(
            oa, ob, oc, od, w_out[layer].astype(BF16), xf, mod, norm_ffn_g[layer][None, :],
            wr_hi, wr_lo, rb.astype(F32), bsz, seq)

        route_t = route.T
        expert = route_t[:, 2:4].astype(jnp.int32)
        block_expert, n_active, dest, n_slots = _dispatch(expert, n)
        dest_t = dest.reshape(n // tc, tc, TOP_K).transpose(0, 2, 1).reshape(n // tc, 1, TOP_K * tc)
        x_slots = _scatter_call(dest_t, h2, n_slots)
        y_slots = _expert_call(block_expert, n_active, x_slots,
                               expert_w_gate, expert_w_up, expert_w_down, layer)
        xf = _combine_call(dest_t, y_slots, x1, route_t, mod, final_norm_g[None, :], bsz, seq,
                           final=(layer == DEPTH - 1))
    return xf.reshape(bsz, seq, d)
```

```python
import functools
import math

import jax
import jax.numpy as jnp
from jax import lax
from jax.experimental import pallas as pl
from jax.experimental.pallas import tpu as pltpu

F32 = jnp.float32
BF16 = jnp.bfloat16

D_MODEL = 1024
DEPTH = 4
CHUNK = 64
EPS = 1e-6
HEAD_DIM = 64
GROUP_WIDTH = 256
MLA_NOPE = 64
MLA_ROPE = 32
MLA_Q_LORA = 192
MLA_KV_LORA = 128
ROPE_THETA = 10000.0
DIFF_HALF = 32
T5_BUCKETS = 32
T5_MAX_DIST = 128
N_GROUPS = 8
EXPERTS_PER_GROUP = 8
N_EXPERTS = 64
TOP_K = 2
D_EXPERT = 512
ADA_CHUNKS = 6

LANES = 128
LOG2E = 1.4426950408889634
NEG = -1e30

_O_MLA = 0
_O_FOX = MLA_Q_LORA + MLA_KV_LORA + MLA_ROPE
_O_CONV = _O_FOX + 3 * GROUP_WIDTH + 4
_O_DIFF = _O_CONV + 3 * GROUP_WIDTH

_CQ_PAD = 256
_W_A = _CQ_PAD + MLA_KV_LORA + 2 * LANES
_W_B = 3 * GROUP_WIDTH
_W_C = 3 * GROUP_WIDTH
_W_D = 3 * GROUP_WIDTH
_P_TOTAL = _W_A + _W_B + _W_C + _W_D

ROW_TILE = 512
ATT_Q = 256
ATT_K = 256
MOE_TILE = 256
CMB_TILE = 512
PIECE = 8
VMEM_LIMIT = 56 * 1024 * 1024


def _cparams(sem):
    return pltpu.CompilerParams(dimension_semantics=sem, vmem_limit_bytes=VMEM_LIMIT)


def _split_bf16(a):
    hi = a.astype(BF16)
    lo = (a - hi.astype(F32)).astype(BF16)
    return hi, lo


def _dot(a, b):
    return jnp.dot(a, b, preferred_element_type=F32)


def _dot_nt(a, b):
    return lax.dot_general(a, b, (((1,), (1,)), ((), ())), preferred_element_type=F32)


def _sigmoid(z):
    return 1.0 / (1.0 + jnp.exp(-z))


def _ada_kernel(c_ref, w_ref, b_ref, o_ref):
    c = c_ref[...]
    cond = c * _sigmoid(c)
    c_hi, c_lo = _split_bf16(cond)
    w_hi, w_lo = _split_bf16(w_ref[0])
    o_ref[0] = _dot(c_hi, w_hi) + _dot(c_hi, w_lo) + _dot(c_lo, w_hi) + b_ref[0]


def _ada_call(c, ada_w, ada_b):
    bsz = c.shape[0]
    n_col = ADA_CHUNKS * D_MODEL // D_MODEL
    return pl.pallas_call(
        _ada_kernel,
        grid=(DEPTH, n_col),
        in_specs=[
            pl.BlockSpec((bsz, D_MODEL), lambda l, j: (0, 0)),
            pl.BlockSpec((1, D_MODEL, D_MODEL), lambda l, j: (l, 0, j)),
            pl.BlockSpec((1, 1, D_MODEL), lambda l, j: (l, 0, j)),
        ],
        out_specs=pl.BlockSpec((1, bsz, D_MODEL), lambda l, j: (l, 0, j)),
        out_shape=jax.ShapeDtypeStruct((DEPTH, bsz, ADA_CHUNKS * D_MODEL), F32),
        compiler_params=_cparams(("arbitrary", "arbitrary")),
        name="ada_mod",
    )(c, ada_w, ada_b.reshape(DEPTH, 1, ADA_CHUNKS * D_MODEL))


def _values_with_ones(v):
    lane = lax.broadcasted_iota(jnp.int32, (1, LANES), 1)
    low_half = lane < HEAD_DIM
    slabs = []
    for head in range(4):
        pair = v[:, (head // 2) * LANES:(head // 2 + 1) * LANES]
        keep = low_half if head % 2 == 0 else jnp.logical_not(low_half)
        slabs.append(jnp.where(keep, pair, 1.0))
    return jnp.concatenate(slabs, axis=-1).astype(BF16)


def _proj_kernel(x_ref, mod_ref, g_ref, w_ref, wf_ref, fb_ref, wuq_ref, wukv_ref, qg_ref, kvg_ref,
                 cos_ref, sin_ref, cw_ref,
                 qa_ref, ka_ref, va_ref, qb_ref, kb_ref, vb_ref, nf_ref, oc_ref, qd_ref, kd_ref, vd_ref,
                 zc_ref, fc_ref, *, scale_a, scale_b, scale_d):
    t = pl.program_id(1)
    rows = x_ref.shape[0]

    @pl.when(t == 0)
    def _():
        zc_ref[...] = jnp.zeros_like(zc_ref)
        fc_ref[...] = jnp.zeros_like(fc_ref)

    x = x_ref[...]
    mod = mod_ref[0]
    hn = x * lax.rsqrt(jnp.mean(x * x, axis=-1, keepdims=True) + EPS) * g_ref[...]
    h = hn * (1.0 + mod[1:2]) + mod[0:1]
    hb = h.astype(BF16)

    z = _dot_nt(wf_ref[...], hb) + fb_ref[...]
    log_f = jnp.minimum(z, 0.0) - jnp.log(1.0 + jnp.exp(-jnp.abs(z)))
    r_i = lax.broadcasted_iota(jnp.int32, (rows, rows), 0)
    c_i = lax.broadcasted_iota(jnp.int32, (rows, rows), 1)
    tri = jnp.where(r_i <= c_i, 1.0, 0.0).astype(BF16)
    f_hi = log_f.astype(BF16)
    rem = log_f - f_hi.astype(F32)
    f_mid = rem.astype(BF16)
    f_lo = (rem - f_mid.astype(F32)).astype(BF16)
    cum = _dot(f_hi, tri) + _dot(f_mid, tri) + _dot(f_lo, tri) + fc_ref[:, 0:1]
    fc_ref[...] = jnp.broadcast_to(cum[:, rows - 1:rows], fc_ref.shape)
    nf_ref[0] = cum[0:8] * (-LOG2E)

    pa = _dot(hb, w_ref[:, 0:_W_A])
    cq = pa[:, 0:_CQ_PAD]
    ckv = pa[:, _CQ_PAD:_CQ_PAD + MLA_KV_LORA]
    kr = pa[:, _CQ_PAD + MLA_KV_LORA:_CQ_PAD + MLA_KV_LORA + LANES]
    krr = pa[:, _CQ_PAD + MLA_KV_LORA + LANES:_W_A]
    cqn = cq * lax.rsqrt(jnp.sum(cq * cq, axis=-1, keepdims=True) * (1.0 / MLA_Q_LORA) + EPS) * qg_ref[...]
    ckvn = ckv * lax.rsqrt(jnp.mean(ckv * ckv, axis=-1, keepdims=True) + EPS) * kvg_ref[...]
    q2 = _dot(cqn.astype(BF16), wuq_ref[...])
    kv2 = _dot(ckvn.astype(BF16), wukv_ref[...])
    cos = cos_ref[...]
    sin = sin_ref[...]
    cos4 = jnp.concatenate([cos] * 4, axis=-1)
    sin4 = jnp.concatenate([sin] * 4, axis=-1)
    qa = (q2[:, 0:512] * cos4 + q2[:, 512:1024] * sin4) * (scale_a * LOG2E)
    k_rope = kr * cos + krr * sin
    ka = kv2[:, 0:512] + jnp.concatenate([k_rope] * 4, axis=-1)
    qa_ref[...] = qa.astype(BF16)
    ka_ref[...] = ka.astype(BF16)
    va_ref[...] = _values_with_ones(kv2[:, 512:768])

    pb = _dot(hb, w_ref[:, _W_A:_W_A + _W_B])
    qb_ref[...] = (pb[:, 0:256] * (scale_b * LOG2E)).astype(BF16)
    kb_ref[...] = pb[:, 256:512].astype(BF16)
    vb_ref[...] = _values_with_ones(pb[:, 512:768])

    pc = _dot(hb, w_ref[:, _W_A + _W_B:_W_A + _W_B + _W_C])
    zz = pc[:, 256:512] * pc[:, 512:768]
    ext = jnp.concatenate([zc_ref[...], zz], axis=0)
    cw = cw_ref[...]
    conv = zz * cw[2:3] + ext[7:rows + 7] * cw[1:2] + ext[6:rows + 6] * cw[0:1]
    zc_ref[...] = zz[rows - 8:rows]
    oc_ref[...] = (pc[:, 0:256] * conv).astype(BF16)

    pd = _dot(hb, w_ref[:, _W_A + _W_B + _W_C:_P_TOTAL])
    qd_ref[...] = (pd[:, 0:256] * (scale_d * LOG2E)).astype(BF16)
    kd_ref[...] = pd[:, 256:512].astype(BF16)
    vd_ref[...] = _values_with_ones(pd[:, 512:768])


def _proj_call(x, mod, g, w, wf, fb, wuq, wukv, qg, kvg, cos_t, sin_t, cw, bsz, seq):
    n = bsz * seq
    rt = min(ROW_TILE, seq)
    nt = seq // rt
    row = lambda width: pl.BlockSpec((rt, width), lambda b, t: (b * nt + t, 0))
    full = lambda a: pl.BlockSpec(a.shape, lambda b, t: (0,) * a.ndim)
    bf = lambda width: jax.ShapeDtypeStruct((n, width), BF16)
    kern = functools.partial(
        _proj_kernel,
        scale_a=(MLA_NOPE + MLA_ROPE) ** -0.5, scale_b=HEAD_DIM ** -0.5, scale_d=DIFF_HALF ** -0.5)
    return pl.pallas_call(
        kern,
        grid=(bsz, nt),
        in_specs=[
            row(D_MODEL),
            pl.BlockSpec((1, ADA_CHUNKS, D_MODEL), lambda b, t: (b, 0, 0)),
            full(g), full(w), full(wf), full(fb), full(wuq), full(wukv), full(qg), full(kvg),
            row(LANES), row(LANES), full(cw),
        ],
        out_specs=[
            row(512), row(512), row(512), row(256), row(256), row(512),
            pl.BlockSpec((1, 8, rt), lambda b, t: (b, 0, t)),
            row(256), row(256), row(256), row(512),
        ],
        out_shape=[
            bf(512), bf(512), bf(512), bf(256), bf(256), bf(512),
            jax.ShapeDtypeStruct((bsz, 8, seq), F32),
            bf(256), bf(256), bf(256), bf(512),
        ],
        scratch_shapes=[pltpu.VMEM((8, GROUP_WIDTH), F32), pltpu.VMEM((16, LANES), F32)],
        compiler_params=_cparams(("arbitrary", "arbitrary")),
        name="norm_in_proj",
    )(x, mod, g, w, wf, fb, wuq, wukv, qg, kvg, cos_t, sin_t, cw)


def _tile_update(q, k_t, v_t, m_ref, acc_ref, bias=None, mask=None):
    s = _dot_nt(q, k_t)
    if bias is not None:
        s = s + bias
    if mask is not None:
        s = jnp.where(mask, s, NEG)
    m_prev = m_ref[...]
    m_new = jnp.maximum(m_prev, jnp.max(s, axis=-1, keepdims=True))
    alpha = jnp.exp2(m_prev - m_new)
    p = jnp.exp2(s - jnp.concatenate([m_new] * (s.shape[1] // LANES), axis=-1))
    acc_ref[...] = alpha * acc_ref[...] + _dot(p.astype(BF16), v_t)
    m_ref[...] = m_new


def _attn_kernel(*refs, kind, tq, tk):
    n_maps = 8 if kind == "D" else 4
    n_in = {"A": 3, "B": 4, "D": 6}[kind]
    q_ref, k_ref, v_ref = refs[0:3]
    o_ref = refs[n_in]
    q_scr = refs[n_in + 1]
    m_refs = refs[n_in + 2:n_in + 2 + n_maps]
    acc_refs = refs[n_in + 2 + n_maps:n_in + 2 + 2 * n_maps]
    if kind == "B":
        nf_ref = refs[3]
    elif kind == "D":
        bias_ref, lam_ref, sg_ref = refs[3:6]
    i = pl.program_id(1)
    ratio = tk // tq
    jd = i // ratio
    par = i % ratio
    off = par * tq
    lane = lax.broadcasted_iota(jnp.int32, (1, LANES), 1)
    low_half = lane < HEAD_DIM
    r_i = lax.broadcasted_iota(jnp.int32, (tq, tk), 0) + off
    c_i = lax.broadcasted_iota(jnp.int32, (tq, tk), 1)
    if kind == "B":
        diag_mask = c_i <= r_i
    else:
        shift = CHUNK.bit_length() - 1
        diag_mask = jnp.right_shift(c_i, shift) <= jnp.right_shift(r_i, shift)

    maps = []
    for head in range(4):
        slab = head // 2
        if kind == "A":
            q_scr[head] = q_ref[:, head * LANES:(head + 1) * LANES]
            maps.append((head, head, head))
        elif kind == "B":
            q = q_ref[:, slab * LANES:(slab + 1) * LANES]
            sel = low_half if head % 2 == 0 else jnp.logical_not(low_half)
            q_scr[head] = jnp.where(sel, q, jnp.zeros_like(q))
            maps.append((head, slab, head))
        else:
            q = q_ref[:, slab * LANES:(slab + 1) * LANES]
            for mp in range(2):
                lo = (2 * (head % 2) + mp) * DIFF_HALF
                sel = (lane >= lo) & (lane < lo + DIFF_HALF)
                q_scr[2 * head + mp] = jnp.where(sel, q, jnp.zeros_like(q))
                maps.append((2 * head + mp, slab, head))
    for idx in range(n_maps):
        m_refs[idx][...] = jnp.full(m_refs[idx].shape, NEG, F32)
        acc_refs[idx][...] = jnp.zeros(acc_refs[idx].shape, F32)

    def step(j, mode):
        start = pl.multiple_of(j * tk, tk)
        for idx, k_slab, head in maps:
            k_t = k_ref[pl.ds(start, tk), k_slab * LANES:(k_slab + 1) * LANES]
            v_t = v_ref[pl.ds(start, tk), head * LANES:(head + 1) * LANES]
            bias = None
            if kind == "B":
                bias = nf_ref[0, j][head:head + 1, :]
            elif kind == "D" and mode != "far":
                bias = bias_ref[head, par, 0 if mode == "diag" else 1]
            _tile_update(q_scr[idx], k_t, v_t, m_refs[idx], acc_refs[idx], bias=bias,
                         mask=diag_mask if mode == "diag" else None)

    def far_pair(jj, carry):
        step(2 * jj, "far")
        step(2 * jj + 1, "far")
        return carry

    n_far = jnp.maximum(jd - 1, 0)
    lax.fori_loop(0, n_far // 2, far_pair, 0)

    @pl.when(n_far % 2 == 1)
    def _():
        step(n_far - 1, "far")

    @pl.when(jd >= 1)
    def _():
        step(jd - 1, "prev")
        step(jd, "diag")

    @pl.when(jd == 0)
    def _():
        step(jd, "diag")

    def normalized(idx):
        acc = acc_refs[idx][...]
        return acc / pltpu.roll(acc, HEAD_DIM, axis=1)

    for slab in range(2):
        outs = []
        for sub in range(2):
            head = 2 * slab + sub
            if kind == "D":
                outs.append(normalized(2 * head) - lam_ref[...] * normalized(2 * head + 1))
            else:
                outs.append(normalized(head))
        o_slab = jnp.where(low_half, outs[0], outs[1])
        if kind == "D":
            sq = o_slab * o_slab
            s_lo = jnp.sum(jnp.where(low_half, sq, 0.0), axis=-1, keepdims=True)
            s_hi = jnp.sum(jnp.where(low_half, 0.0, sq), axis=-1, keepdims=True)
            ms = jnp.where(low_half, s_lo, s_hi) * (1.0 / HEAD_DIM)
            o_slab = o_slab * lax.rsqrt(ms + EPS) * sg_ref[...]
        o_ref[:, slab * LANES:(slab + 1) * LANES] = o_slab.astype(BF16)


def _attn_call(kind, q, k, v, extra, bsz, seq):
    n = bsz * seq
    tq = min(ATT_Q, seq)
    tk = min(ATT_K, seq)
    nq = seq // tq
    wq = q.shape[1]
    in_specs = [
        pl.BlockSpec((tq, wq), lambda b, i: (b * nq + i, 0)),
        pl.BlockSpec((seq, k.shape[1]), lambda b, i: (b, 0)),
        pl.BlockSpec((seq, v.shape[1]), lambda b, i: (b, 0)),
    ]
    if kind == "B":
        in_specs.append(pl.BlockSpec((1, seq // tk, 8, tk), lambda b, i: (b, 0, 0, 0)))
    elif kind == "D":
        bias, lam, sg = extra
        in_specs += [
            pl.BlockSpec(bias.shape, lambda b, i: (0,) * bias.ndim),
            pl.BlockSpec(lam.shape, lambda b, i: (0, 0)),
            pl.BlockSpec(sg.shape, lambda b, i: (0, 0)),
        ]
    n_maps = 8 if kind == "D" else 4
    return pl.pallas_call(
        functools.partial(_attn_kernel, kind=kind, tq=tq, tk=tk),
        grid=(bsz, nq),
        in_specs=in_specs,
        out_specs=pl.BlockSpec((tq, GROUP_WIDTH), lambda b, i: (b * nq + i, 0)),
        out_shape=jax.ShapeDtypeStruct((n, GROUP_WIDTH), BF16),
        scratch_shapes=(
            [pltpu.VMEM((n_maps, tq, LANES), BF16)]
            + [pltpu.VMEM((tq, LANES), F32) for _ in range(2 * n_maps)]
        ),
        compiler_params=_cparams(("arbitrary", "arbitrary")),
        name="attn_" + kind,
    )(q, k, v, *extra)


def _out_router_kernel(oa_ref, ob_ref, oc_ref, od_ref, wo_ref, x_ref, mod_ref, g_ref, wr_hi_ref, wr_lo_ref,
                       rb_ref, xo_ref, h2_ref, route_ref):
    gw = GROUP_WIDTH
    mix = (_dot(oa_ref[...], wo_ref[0:gw]) + _dot(ob_ref[...], wo_ref[gw:2 * gw])
           + _dot(oc_ref[...], wo_ref[2 * gw:3 * gw]) + _dot(od_ref[...], wo_ref[3 * gw:4 * gw]))
    mod = mod_ref[0]
    x1 = x_ref[...] + mod[2:3] * mix
    xo_ref[...] = x1
    hn = x1 * lax.rsqrt(jnp.mean(x1 * x1, axis=-1, keepdims=True) + EPS) * g_ref[...]
    h2 = hn * (1.0 + mod[4:5]) + mod[3:4]
    h2_ref[...] = h2

    a_hi, a_lo = _split_bf16(h2)
    w_hi = wr_hi_ref[...]
    lg = _dot_nt(w_hi, a_hi) + _dot_nt(w_hi, a_lo) + _dot_nt(wr_lo_ref[...], a_hi) + rb_ref[...]
    gl = lg[0:N_GROUPS]
    rows = gl.shape[1]
    iota8 = lax.broadcasted_iota(jnp.int32, (N_GROUPS, rows), 0)
    gmax = jnp.max(gl, axis=0, keepdims=True)
    grp = jnp.min(jnp.where(gl == gmax, iota8, N_GROUPS), axis=0, keepdims=True)
    p_grp = 1.0 / jnp.sum(jnp.exp(gl - gmax), axis=0, keepdims=True)
    esel = jnp.zeros((EXPERTS_PER_GROUP, rows), F32)
    for gi in range(N_GROUPS):
        lo = N_GROUPS + gi * EXPERTS_PER_GROUP
        esel = jnp.where(grp == gi, lg[lo:lo + EXPERTS_PER_GROUP], esel)
    ee = jnp.exp(esel - jnp.max(esel, axis=0, keepdims=True))
    ps = ee / jnp.sum(ee, axis=0, keepdims=True)
    p1 = jnp.max(ps, axis=0, keepdims=True)
    i1 = jnp.min(jnp.where(ps == p1, iota8, EXPERTS_PER_GROUP), axis=0, keepdims=True)
    ps2 = jnp.where(iota8 == i1, -1.0, ps)
    p2 = jnp.max(ps2, axis=0, keepdims=True)
    i2 = jnp.min(jnp.where(ps2 == p2, iota8, EXPERTS_PER_GROUP), axis=0, keepdims=True)
    den = p1 + p2
    g1 = p_grp * p1 / den
    g2 = p_grp * p2 / den
    e1 = (grp * EXPERTS_PER_GROUP + i1).astype(F32)
    e2 = (grp * EXPERTS_PER_GROUP + i2).astype(F32)
    zero = jnp.zeros_like(g1)
    route_ref[...] = jnp.concatenate([g1, g2, e1, e2, zero, zero, zero, zero], axis=0)


def _out_router_call(oa, ob, oc, od, wo, x, mod, g, wr_hi, wr_lo, rb, bsz, seq):
    n = bsz * seq
    rt = min(ROW_TILE, seq)
    nt = seq // rt
    row = lambda width: pl.BlockSpec((rt, width), lambda t: (t, 0))
    full = lambda a: pl.BlockSpec(a.shape, lambda t: (0,) * a.ndim)
    return pl.pallas_call(
        _out_router_kernel,
        grid=(n // rt,),
        in_specs=[
            row(GROUP_WIDTH), row(GROUP_WIDTH), row(GROUP_WIDTH), row(GROUP_WIDTH), full(wo),
            row(D_MODEL),
            pl.BlockSpec((1, ADA_CHUNKS, D_MODEL), lambda t: (t // nt, 0, 0)),
            full(g), full(wr_hi), full(wr_lo), full(rb),
        ],
        out_specs=[row(D_MODEL), row(D_MODEL), pl.BlockSpec((8, rt), lambda t: (0, t))],
        out_shape=[
            jax.ShapeDtypeStruct((n, D_MODEL), F32),
            jax.ShapeDtypeStruct((n, D_MODEL), F32),
            jax.ShapeDtypeStruct((8, n), F32),
        ],
        compiler_params=_cparams(("arbitrary",)),
        name="out_proj_router",
    )(oa, ob, oc, od, wo, x, mod, g, wr_hi, wr_lo, rb)


def _dispatch_kernel(np_ref, slot_ref, lp_ref, h_ref, init_hbm, xs_hbm, sorted_buf, sem):
    del init_hbm
    t = pl.program_id(0)
    rows_l = sorted_buf.shape[0]
    lp = lp_ref[0]
    pos = lax.broadcasted_iota(jnp.int32, (rows_l, lp.shape[1]), 0)
    hit = pos == lp[0:1, :]
    for k in range(1, TOP_K):
        hit = hit | (pos == lp[k:k + 1, :])
    perm = jnp.where(hit, 1.0, 0.0).astype(BF16)
    sorted_buf[...] = _dot(perm, h_ref[...].astype(BF16))

    def piece(q):
        src = sorted_buf.at[pl.ds(pl.multiple_of(q * PIECE, PIECE), PIECE)]
        slot = pl.multiple_of(slot_ref[0, 0, q], PIECE)
        return pltpu.make_async_copy(src, xs_hbm.at[pl.ds(slot, PIECE)], sem)

    def issue(q, carry):
        piece(q).start()
        return carry

    def drain(q, carry):
        piece(q).wait()
        return carry
    lax.fori_loop(0, np_ref[t], issue, 0)
    lax.fori_loop(0, np_ref[t], drain, 0)


def _dispatch_call(n_pieces, piece_slot, lp_rows, h2, n_slots):
    n = h2.shape[0]
    n_tiles, _, q_max = piece_slot.shape
    td = n // n_tiles
    init = jnp.zeros((n_slots, D_MODEL), F32)
    grid_spec = pltpu.PrefetchScalarGridSpec(
        num_scalar_prefetch=1,
        grid=(n_tiles,),
        in_specs=[
            pl.BlockSpec((1, 1, q_max), lambda t, npc: (t, 0, 0), memory_space=pltpu.SMEM),
            pl.BlockSpec((1, 8, td), lambda t, npc: (t, 0, 0)),
            pl.BlockSpec((td, D_MODEL), lambda t, npc: (t, 0)),
            pl.BlockSpec(memory_space=pl.ANY),
        ],
        out_specs=pl.BlockSpec(memory_space=pl.ANY),
        scratch_shapes=[pltpu.VMEM((q_max * PIECE, D_MODEL), F32), pltpu.SemaphoreType.DMA],
    )
    return pl.pallas_call(
        _dispatch_kernel,
        grid_spec=grid_spec,
        out_shape=jax.ShapeDtypeStruct((n_slots, D_MODEL), F32),
        input_output_aliases={4: 0},
        compiler_params=_cparams(("arbitrary",)),
        name="moe_dispatch",
    )(n_pieces, piece_slot, lp_rows, h2, init)


def _expert_kernel(be_ref, nact_ref, x_ref, wg_ref, wu_ref, wd_ref, y_ref, wgb, wub, wdb):
    i = pl.program_id(0)

    @pl.when(i < nact_ref[0])
    def _():
        prev = be_ref[jnp.maximum(i - 1, 0)]

        @pl.when((i == 0) | (be_ref[i] != prev))
        def _():
            wgb[...] = wg_ref[0, 0].astype(BF16)
            wub[...] = wu_ref[0, 0].astype(BF16)
            wdb[...] = wd_ref[0, 0].astype(BF16)

        xb = x_ref[...].astype(BF16)
        gate = _dot(xb, wgb[...])
        up = _dot(xb, wub[...])
        hid = gate * _sigmoid(gate) * up
        y_ref[...] = _dot(hid.astype(BF16), wdb[...])

    @pl.when(i >= nact_ref[0])
    def _():
        y_ref[...] = jnp.zeros_like(y_ref)


def _expert_call(block_expert, n_active, x_slots, w_gate, w_up, w_down, layer):
    n_blocks = block_expert.shape[0]
    tm = MOE_TILE
    w_idx = lambda i, be, na: (layer, be[i], 0, 0)
    grid_spec = pltpu.PrefetchScalarGridSpec(
        num_scalar_prefetch=2,
        grid=(n_blocks,),
        in_specs=[
            pl.BlockSpec((tm, D_MODEL), lambda i, be, na: (jnp.minimum(i, na[0] - 1), 0)),
            pl.BlockSpec((1, 1, D_MODEL, D_EXPERT), w_idx),
            pl.BlockSpec((1, 1, D_MODEL, D_EXPERT), w_idx),
            pl.BlockSpec((1, 1, D_EXPERT, D_MODEL), w_idx),
        ],
        out_specs=pl.BlockSpec((tm, D_MODEL), lambda i, be, na: (i, 0)),
        scratch_shapes=[
            pltpu.VMEM((D_MODEL, D_EXPERT), BF16),
            pltpu.VMEM((D_MODEL, D_EXPERT), BF16),
            pltpu.VMEM((D_EXPERT, D_MODEL), BF16),
        ],
    )
    return pl.pallas_call(
        _expert_kernel,
        grid_spec=grid_spec,
        out_shape=jax.ShapeDtypeStruct((n_blocks * tm, D_MODEL), F32),
        compiler_params=_cparams(("arbitrary",)),
        name="expert_mlp",
    )(block_expert, n_active, x_slots, w_gate, w_up, w_down)


def _combine_kernel(slot_ref, y_hbm, x_ref, route_ref, mod_ref, fg_ref, o_ref, ybuf, sem, *, final):
    n_q = slot_ref.shape[2]

    def issue(q, carry):
        dst = ybuf.at[pl.ds(pl.multiple_of(q * PIECE, PIECE), PIECE)]
        slot = pl.multiple_of(slot_ref[0, 0, q], PIECE)
        pltpu.make_async_copy(y_hbm.at[pl.ds(slot, PIECE)], dst, sem).start()
        return carry
    lax.fori_loop(0, n_q, issue, 0)
    pltpu.make_async_copy(y_hbm.at[pl.ds(0, ybuf.shape[0])], ybuf, sem).wait()

    rt = route_ref[...]
    pos = lax.broadcasted_iota(jnp.int32, (rt.shape[0], ybuf.shape[0]), 1)
    weights = jnp.zeros(pos.shape, F32)
    for k in range(TOP_K):
        lp_k = rt[:, 4 + k:5 + k].astype(jnp.int32)
        weights = jnp.where(pos == lp_k, rt[:, k:k + 1], weights)
    y = _dot(weights.astype(BF16), ybuf[...].astype(BF16))
    x2 = x_ref[...] + mod_ref[0][5:6] * y
    if final:
        x2 = x2 * lax.rsqrt(jnp.mean(x2 * x2, axis=-1, keepdims=True) + EPS) * fg_ref[...]
    o_ref[...] = x2


def _combine_call(piece_slot, y_slots, x, route_lp, mod, fg, bsz, seq, final):
    n = bsz * seq
    n_tiles, _, q_max = piece_slot.shape
    tc = n // n_tiles
    nt = seq // tc
    return pl.pallas_call(
        functools.partial(_combine_kernel, final=final),
        grid=(n_tiles,),
        in_specs=[
            pl.BlockSpec((1, 1, q_max), lambda t: (t, 0, 0), memory_space=pltpu.SMEM),
            pl.BlockSpec(memory_space=pl.ANY),
            pl.BlockSpec((tc, D_MODEL), lambda t: (t, 0)),
            pl.BlockSpec((tc, 8), lambda t: (t, 0)),
            pl.BlockSpec((1, ADA_CHUNKS, D_MODEL), lambda t: (t // nt, 0, 0)),
            pl.BlockSpec((1, D_MODEL), lambda t: (0, 0)),
        ],
        out_specs=pl.BlockSpec((tc, D_MODEL), lambda t: (t, 0)),
        out_shape=jax.ShapeDtypeStruct((n, D_MODEL), F32),
        scratch_shapes=[pltpu.VMEM((q_max * PIECE, D_MODEL), F32), pltpu.SemaphoreType.DMA],
        compiler_params=_cparams(("arbitrary",)),
        name="moe_combine",
    )(piece_slot, y_slots, x, route_lp, mod, fg)


def _t5_bucket(rel):
    nb = T5_BUCKETS // 2
    max_exact = nb // 2
    bucket = jnp.where(rel > 0, nb, 0)
    n = jnp.abs(rel)
    large = max_exact + (jnp.log(jnp.maximum(n, 1).astype(F32) / max_exact)
                         / math.log(T5_MAX_DIST / max_exact) * (nb - max_exact)).astype(jnp.int32)
    large = jnp.minimum(large, nb - 1)
    return bucket + jnp.where(n < max_exact, n, large)


def _t5_tiles(t5_table, tq, tk):
    assert tk >= T5_MAX_DIST and tk % tq == 0
    par = jnp.arange(tk // tq, dtype=jnp.int32)[:, None, None, None]
    back = jnp.arange(2, dtype=jnp.int32)[None, :, None, None]
    r = jnp.arange(tq, dtype=jnp.int32)[None, None, :, None]
    c = jnp.arange(tk, dtype=jnp.int32)[None, None, None, :]
    rel = c - back * tk - (r + par * tq)
    far = t5_table[_t5_bucket(jnp.array(-4 * T5_MAX_DIST, jnp.int32))]
    b = t5_table[_t5_bucket(rel)] - far
    return (b.transpose(4, 0, 1, 2, 3) * LOG2E).astype(F32)


def _place(cols, width, offset):
    z = jnp.zeros((cols.shape[0], width), cols.dtype)
    return z.at[:, offset:offset + cols.shape[1]].set(cols)


def _rot_half(cols):
    half = cols.shape[1] // 2
    return jnp.concatenate([-cols[:, half:], cols[:, :half]], axis=1)


def _prep_in_proj(w_in):
    cq = _place(w_in[:, 0:MLA_Q_LORA], _CQ_PAD, 0)
    ckv = w_in[:, MLA_Q_LORA:MLA_Q_LORA + MLA_KV_LORA]
    kr = w_in[:, MLA_Q_LORA + MLA_KV_LORA:_O_FOX]
    kr_p = _place(kr, LANES, MLA_NOPE)
    kr_rot = _place(_rot_half(kr), LANES, MLA_NOPE)
    fox = w_in[:, _O_FOX:_O_FOX + 3 * GROUP_WIDTH]
    wf = w_in[:, _O_FOX + 3 * GROUP_WIDTH:_O_CONV]
    conv = w_in[:, _O_CONV:_O_DIFF]
    diff = w_in[:, _O_DIFF:_O_DIFF + 3 * GROUP_WIDTH]
    w = jnp.concatenate([cq, ckv, kr_p, kr_rot, fox, conv, diff], axis=1).astype(BF16)
    wf_t = _place(wf, 16, 0).T.astype(BF16)
    return w, wf_t


def _prep_mla(w_uq, w_ukv):
    hd = MLA_NOPE + MLA_ROPE
    plain, rot, k_nope, v = [], [], [], []
    for h in range(4):
        wh = w_uq[:, h * hd:(h + 1) * hd]
        plain.append(_place(wh, LANES, 0))
        rot.append(_place(_rot_half(wh[:, MLA_NOPE:]), LANES, MLA_NOPE))
        kv = w_ukv[:, h * 2 * HEAD_DIM:(h + 1) * 2 * HEAD_DIM]
        k_nope.append(_place(kv[:, :MLA_NOPE], LANES, 0))
        v.append(kv[:, MLA_NOPE:])
    wuq = jnp.concatenate(plain + rot, axis=1)
    wuq = jnp.concatenate([wuq, jnp.zeros((_CQ_PAD - MLA_Q_LORA, wuq.shape[1]), wuq.dtype)], axis=0)
    wukv = jnp.concatenate(k_nope + v, axis=1)
    return wuq.astype(BF16), wukv.astype(BF16)


def _rope_tables(positions):
    half = MLA_ROPE // 2
    inv_freq = ROPE_THETA ** (-jnp.arange(half, dtype=F32) / half)
    ang = positions.astype(F32).reshape(-1)[:, None] * inv_freq
    cos, sin = jnp.cos(ang), jnp.sin(ang)
    n = ang.shape[0]
    cos_t = jnp.ones((n, LANES), F32).at[:, MLA_NOPE:MLA_NOPE + MLA_ROPE].set(jnp.concatenate([cos, cos], 1))
    sin_t = jnp.zeros((n, LANES), F32).at[:, MLA_NOPE:MLA_NOPE + MLA_ROPE].set(jnp.concatenate([sin, sin], 1))
    return cos_t, sin_t


def _piece_rows(tile):
    rows = tile * TOP_K + N_EXPERTS * (PIECE - 1)
    return -(-rows // LANES) * LANES


def _dispatch(expert, n_tok, tile):
    tm = MOE_TILE
    n_assign = n_tok * TOP_K
    per_tile = tile * TOP_K
    n_tiles = n_tok // tile
    q_max = _piece_rows(tile) // PIECE
    chunk = LANES
    experts = jnp.arange(N_EXPERTS, dtype=jnp.int32)
    e_flat = expert.reshape(n_assign)
    onehot = e_flat[:, None] == experts[None, :]
    oh = onehot.astype(BF16).reshape(n_assign // chunk, chunk, N_EXPERTS)
    tri = (jnp.arange(chunk)[:, None] >= jnp.arange(chunk)[None, :]).astype(BF16)
    within = jnp.einsum("ij,cjk->cik", tri, oh, preferred_element_type=F32)
    total = within[:, chunk - 1, :]
    before = jnp.cumsum(total, axis=0) - total
    running = (within + before[:, None, :]).reshape(n_tiles, per_tile, N_EXPERTS)
    onehot = onehot.reshape(n_tiles, per_tile, N_EXPERTS)

    run_end = running[:, per_tile - 1, :]
    run_start = jnp.concatenate([jnp.zeros((1, N_EXPERTS), F32), run_end[:-1]], axis=0)
    pieces = ((run_end - run_start).astype(jnp.int32) + PIECE - 1) // PIECE
    run_rows = pieces * PIECE
    rows_before = jnp.cumsum(run_rows, axis=0) - run_rows
    counts = rows_before[-1] + run_rows[-1]
    padded = (counts + tm - 1) // tm * tm
    pad_ends = jnp.cumsum(padded)
    pad_starts = pad_ends - padded

    piece_end = jnp.cumsum(pieces, axis=1)
    piece_first = piece_end - pieces
    local_start = (piece_first * PIECE).astype(F32)
    local = running - 1.0 - run_start[:, None, :] + local_start[:, None, :]
    lp = jnp.sum(jnp.where(onehot, local, 0.0), axis=2).astype(jnp.int32)
    lp = lp.reshape(n_tok, TOP_K)

    q = jnp.arange(q_max, dtype=jnp.int32)[None, :]
    e_of_q = jnp.sum((piece_end[:, None, :] <= q[:, :, None]).astype(jnp.int32), axis=2)
    pick = e_of_q[:, :, None] == experts[None, None, :]
    run_slot = pad_starts[None, :] + rows_before
    first_q = jnp.sum(jnp.where(pick, piece_first[:, None, :], 0), axis=2)
    slot_q = jnp.sum(jnp.where(pick, run_slot[:, None, :], 0), axis=2)
    piece_slot = jnp.where(e_of_q < N_EXPERTS, slot_q + (q - first_q) * PIECE, 0)
    n_pieces = piece_end[:, N_EXPERTS - 1]

    n_blocks = -(-(n_assign + N_EXPERTS * (n_tiles * (PIECE - 1) + tm - 1)) // tm)
    block_start = jnp.arange(n_blocks, dtype=jnp.int32) * tm
    block_expert = jnp.minimum(
        jnp.sum((pad_ends[None, :] <= block_start[:, None]).astype(jnp.int32), axis=1), N_EXPERTS - 1)
    n_active = (pad_ends[-1] // tm).astype(jnp.int32).reshape(1)
    return (block_expert, n_active, lp, piece_slot.reshape(n_tiles, 1, q_max).astype(jnp.int32),
            n_pieces.astype(jnp.int32), n_blocks * tm)


def kernel(x, c, positions, t5_table, ada_w, ada_b, norm_mix_g, norm_ffn_g, w_in, mla_q_norm_g, mla_w_uq, mla_kv_norm_g, mla_w_ukv, fox_forget_b, conv_w, diff_lambda, diff_subln_g, w_out, router_group_w, router_group_b, router_expert_w, router_expert_b, expert_w_gate, expert_w_up, expert_w_down, final_norm_g):
    bsz, seq, d = x.shape
    n = bsz * seq
    tq = min(ATT_Q, seq)
    tk = min(ATT_K, seq)
    tc = min(CMB_TILE, seq)

    mods = _ada_call(c, ada_w, ada_b).reshape(DEPTH, bsz, ADA_CHUNKS, D_MODEL)
    cos_t, sin_t = _rope_tables(positions)
    bias_tiles = _t5_tiles(t5_table, tq, tk)
    xf = x.reshape(n, d)

    for layer in range(DEPTH):
        mod = mods[layer]
        w, wf_t = _prep_in_proj(w_in[layer])
        wuq, wukv = _prep_mla(mla_w_uq[layer], mla_w_ukv[layer])
        fb = _place(fox_forget_b[layer][None, :], 16, 0).T.astype(F32)
        qg = _place(mla_q_norm_g[layer][None, :], _CQ_PAD, 0)
        kvg = mla_kv_norm_g[layer][None, :]
        cw = _place(conv_w[layer].T, 8, 0).T
        (qa, ka, va, qb, kb, vb, nf, oc, qd, kd, vd) = _proj_call(
            xf, mod, norm_mix_g[layer][None, :], w, wf_t, fb, wuq, wukv, qg, kvg, cos_t, sin_t, cw, bsz, seq)

        oa = _attn_call("A", qa, ka, va, (), bsz, seq)
        nf_t = nf.reshape(bsz, 8, seq // tk, tk).transpose(0, 2, 1, 3)
        ob = _attn_call("B", qb, kb, vb, (nf_t,), bsz, seq)
        lam_init = 0.8 - 0.6 * math.exp(-0.3 * layer)
        lp = diff_lambda[layer].astype(F32)
        lam = jnp.exp(jnp.sum(lp[0] * lp[1])) - jnp.exp(jnp.sum(lp[2] * lp[3])) + lam_init
        lam_row = jnp.full((1, LANES), lam, F32)
        sg = (jnp.concatenate([diff_subln_g[layer]] * 2) * (1.0 - lam_init))[None, :].astype(F32)
        od = _attn_call("D", qd, kd, vd, (bias_tiles, lam_row, sg), bsz, seq)

        wr = jnp.concatenate([router_group_w[layer], router_expert_w[layer]], axis=1)
        wr_t = _place(wr, 80, 0).T
        wr_hi = wr_t.astype(BF16)
        wr_lo = (wr_t - wr_hi.astype(F32)).astype(BF16)
        rb = _place(jnp.concatenate([router_group_b[layer], router_expert_b[layer]])[None, :], 80, 0).T
        x1, h2, route = _out_router_call(
            oa, ob, oc, od, w_out[layer].astype(BF16), xf, mod, norm_ffn_g[layer][None, :],
            wr_hi, wr_lo, rb.astype(F32), bsz, seq)

        route_t = route.T
        expert = route_t[:, 2:4].astype(jnp.int32)
        block_expert, n_active, lp, piece_slot, n_pieces, n_slots = _dispatch(expert, n, tc)
        lp_rows = jnp.full((n // tc, 8, tc), -1, jnp.int32).at[:, 0:TOP_K, :].set(
            lp.reshape(n // tc, tc, TOP_K).transpose(0, 2, 1))
        x_slots = _dispatch_call(n_pieces, piece_slot, lp_rows, h2, n_slots)
        y_slots = _expert_call(block_expert, n_active, x_slots,
                               expert_w_gate, expert_w_up, expert_w_down, layer)
        route_lp = jnp.concatenate([route_t[:, 0:4], lp.astype(F32), route_t[:, 6:8]], axis=1)
        xf = _combine_call(piece_slot, y_slots, x1, route_lp, mod, final_norm_g[None, :], bsz, seq,
                           final=(layer == DEPTH - 1))
    return xf.reshape(bsz, seq, d)
```

```python
import functools
import math

import jax
import jax.numpy as jnp
from jax import lax
from jax.experimental import pallas as pl
from jax.experimental.pallas import tpu as pltpu

F32 = jnp.float32
BF16 = jnp.bfloat16

D_MODEL = 1024
DEPTH = 4
CHUNK = 64
EPS = 1e-6
HEAD_DIM = 64
GROUP_WIDTH = 256
MLA_NOPE = 64
MLA_ROPE = 32
MLA_Q_LORA = 192
MLA_KV_LORA = 128
ROPE_THETA = 10000.0
DIFF_HALF = 32
T5_BUCKETS = 32
T5_MAX_DIST = 128
N_GROUPS = 8
EXPERTS_PER_GROUP = 8
N_EXPERTS = 64
TOP_K = 2
D_EXPERT = 512
ADA_CHUNKS = 6

LANES = 128
LOG2E = 1.4426950408889634
NEG = -1e30

_O_MLA = 0
_O_FOX = MLA_Q_LORA + MLA_KV_LORA + MLA_ROPE
_O_CONV = _O_FOX + 3 * GROUP_WIDTH + 4
_O_DIFF = _O_CONV + 3 * GROUP_WIDTH

_CQ_PAD = 256
_W_A = _CQ_PAD + MLA_KV_LORA + 2 * LANES
_W_B = 3 * GROUP_WIDTH
_W_C = 3 * GROUP_WIDTH
_W_D = 3 * GROUP_WIDTH
_P_TOTAL = _W_A + _W_B + _W_C + _W_D

ROW_TILE = 512
ATT_Q = 256
ATT_K = 256
MOE_TILE = 256
CMB_TILE = 512
PIECE = 8
VMEM_LIMIT = 56 * 1024 * 1024


def _cparams(sem):
    return pltpu.CompilerParams(dimension_semantics=sem, vmem_limit_bytes=VMEM_LIMIT)


def _split_bf16(a):
    hi = a.astype(BF16)
    lo = (a - hi.astype(F32)).astype(BF16)
    return hi, lo


def _dot(a, b):
    return jnp.dot(a, b, preferred_element_type=F32)


def _dot_nt(a, b):
    return lax.dot_general(a, b, (((1,), (1,)), ((), ())), preferred_element_type=F32)


def _sigmoid(z):
    return 1.0 / (1.0 + jnp.exp(-z))


_HIGH16 = -65536


def _pack_bf16_pair(lo, hi):
    lo_bits = pltpu.bitcast(lo.astype(BF16).astype(F32), jnp.int32)
    hi_bits = pltpu.bitcast(hi.astype(BF16).astype(F32), jnp.int32)
    sixteen = jnp.full(lo_bits.shape, 16, jnp.int32)
    return (hi_bits & _HIGH16) | lax.shift_right_logical(lo_bits, sixteen)


def _unpack_bf16_pair(packed):
    lo = pltpu.bitcast(lax.shift_left(packed, jnp.full(packed.shape, 16, jnp.int32)), F32)
    hi = pltpu.bitcast(packed & _HIGH16, F32)
    return lo.astype(BF16), hi.astype(BF16)


def _ada_kernel(c_ref, w_ref, b_ref, o_ref):
    c = c_ref[...]
    cond = c * _sigmoid(c)
    c_hi, c_lo = _split_bf16(cond)
    w_hi, w_lo = _split_bf16(w_ref[0])
    o_ref[0] = _dot(c_hi, w_hi) + _dot(c_hi, w_lo) + _dot(c_lo, w_hi) + b_ref[0]


def _ada_call(c, ada_w, ada_b):
    bsz = c.shape[0]
    n_col = ADA_CHUNKS * D_MODEL // D_MODEL
    return pl.pallas_call(
        _ada_kernel,
        grid=(DEPTH, n_col),
        in_specs=[
            pl.BlockSpec((bsz, D_MODEL), lambda l, j: (0, 0)),
            pl.BlockSpec((1, D_MODEL, D_MODEL), lambda l, j: (l, 0, j)),
            pl.BlockSpec((1, 1, D_MODEL), lambda l, j: (l, 0, j)),
        ],
        out_specs=pl.BlockSpec((1, bsz, D_MODEL), lambda l, j: (l, 0, j)),
        out_shape=jax.ShapeDtypeStruct((DEPTH, bsz, ADA_CHUNKS * D_MODEL), F32),
        compiler_params=_cparams(("arbitrary", "arbitrary")),
        name="ada_mod",
    )(c, ada_w, ada_b.reshape(DEPTH, 1, ADA_CHUNKS * D_MODEL))


def _values_with_ones(v):
    lane = lax.broadcasted_iota(jnp.int32, (1, LANES), 1)
    low_half = lane < HEAD_DIM
    slabs = []
    for head in range(4):
        pair = v[:, (head // 2) * LANES:(head // 2 + 1) * LANES]
        keep = low_half if head % 2 == 0 else jnp.logical_not(low_half)
        slabs.append(jnp.where(keep, pair, 1.0))
    return jnp.concatenate(slabs, axis=-1).astype(BF16)


def _proj_kernel(x_ref, mod_ref, g_ref, w_ref, wf_ref, fb_ref, wuq_ref, wukv_ref, qg_ref, kvg_ref,
                 cos_ref, sin_ref, cw_ref,
                 qa_ref, ka_ref, va_ref, qb_ref, kb_ref, vb_ref, nf_ref, oc_ref, qd_ref, kd_ref, vd_ref,
                 zc_ref, fc_ref, *, scale_a, scale_b, scale_d):
    t = pl.program_id(1)
    rows = x_ref.shape[0]

    @pl.when(t == 0)
    def _():
        zc_ref[...] = jnp.zeros_like(zc_ref)
        fc_ref[...] = jnp.zeros_like(fc_ref)

    x = x_ref[...]
    mod = mod_ref[0]
    hn = x * lax.rsqrt(jnp.mean(x * x, axis=-1, keepdims=True) + EPS) * g_ref[...]
    h = hn * (1.0 + mod[1:2]) + mod[0:1]
    hb = h.astype(BF16)

    z = _dot_nt(wf_ref[...], hb) + fb_ref[...]
    log_f = jnp.minimum(z, 0.0) - jnp.log(1.0 + jnp.exp(-jnp.abs(z)))
    r_i = lax.broadcasted_iota(jnp.int32, (rows, rows), 0)
    c_i = lax.broadcasted_iota(jnp.int32, (rows, rows), 1)
    tri = jnp.where(r_i <= c_i, 1.0, 0.0).astype(BF16)
    f_hi = log_f.astype(BF16)
    rem = log_f - f_hi.astype(F32)
    f_mid = rem.astype(BF16)
    f_lo = (rem - f_mid.astype(F32)).astype(BF16)
    cum = _dot(f_hi, tri) + _dot(f_mid, tri) + _dot(f_lo, tri) + fc_ref[:, 0:1]
    fc_ref[...] = jnp.broadcast_to(cum[:, rows - 1:rows], fc_ref.shape)
    nf_ref[0] = cum[0:8] * (-LOG2E)

    pa = _dot(hb, w_ref[:, 0:_W_A])
    cq = pa[:, 0:_CQ_PAD]
    ckv = pa[:, _CQ_PAD:_CQ_PAD + MLA_KV_LORA]
    kr = pa[:, _CQ_PAD + MLA_KV_LORA:_CQ_PAD + MLA_KV_LORA + LANES]
    krr = pa[:, _CQ_PAD + MLA_KV_LORA + LANES:_W_A]
    cqn = cq * lax.rsqrt(jnp.sum(cq * cq, axis=-1, keepdims=True) * (1.0 / MLA_Q_LORA) + EPS) * qg_ref[...]
    ckvn = ckv * lax.rsqrt(jnp.mean(ckv * ckv, axis=-1, keepdims=True) + EPS) * kvg_ref[...]
    q2 = _dot(cqn.astype(BF16), wuq_ref[...])
    kv2 = _dot(ckvn.astype(BF16), wukv_ref[...])
    cos = cos_ref[...]
    sin = sin_ref[...]
    cos4 = jnp.concatenate([cos] * 4, axis=-1)
    sin4 = jnp.concatenate([sin] * 4, axis=-1)
    qa = (q2[:, 0:512] * cos4 + q2[:, 512:1024] * sin4) * (scale_a * LOG2E)
    k_rope = kr * cos + krr * sin
    ka = kv2[:, 0:512] + jnp.concatenate([k_rope] * 4, axis=-1)
    qa_ref[...] = qa.astype(BF16)
    ka_ref[...] = ka.astype(BF16)
    va_ref[...] = _values_with_ones(kv2[:, 512:768])

    pb = _dot(hb, w_ref[:, _W_A:_W_A + _W_B])
    qb_ref[...] = (pb[:, 0:256] * (scale_b * LOG2E)).astype(BF16)
    kb_ref[...] = pb[:, 256:512].astype(BF16)
    vb_ref[...] = _values_with_ones(pb[:, 512:768])

    pc = _dot(hb, w_ref[:, _W_A + _W_B:_W_A + _W_B + _W_C])
    zz = pc[:, 256:512] * pc[:, 512:768]
    ext = jnp.concatenate([zc_ref[...], zz], axis=0)
    cw = cw_ref[...]
    conv = zz * cw[2:3] + ext[7:rows + 7] * cw[1:2] + ext[6:rows + 6] * cw[0:1]
    zc_ref[...] = zz[rows - 8:rows]
    oc_ref[...] = (pc[:, 0:256] * conv).astype(BF16)

    pd = _dot(hb, w_ref[:, _W_A + _W_B + _W_C:_P_TOTAL])
    qd_ref[...] = (pd[:, 0:256] * (scale_d * LOG2E)).astype(BF16)
    kd_ref[...] = pd[:, 256:512].astype(BF16)
    vd_ref[...] = _values_with_ones(pd[:, 512:768])


def _proj_call(x, mod, g, w, wf, fb, wuq, wukv, qg, kvg, cos_t, sin_t, cw, bsz, seq):
    n = bsz * seq
    rt = min(ROW_TILE, seq)
    nt = seq // rt
    row = lambda width: pl.BlockSpec((rt, width), lambda b, t: (b * nt + t, 0))
    full = lambda a: pl.BlockSpec(a.shape, lambda b, t: (0,) * a.ndim)
    bf = lambda width: jax.ShapeDtypeStruct((n, width), BF16)
    kern = functools.partial(
        _proj_kernel,
        scale_a=(MLA_NOPE + MLA_ROPE) ** -0.5, scale_b=HEAD_DIM ** -0.5, scale_d=DIFF_HALF ** -0.5)
    return pl.pallas_call(
        kern,
        grid=(bsz, nt),
        in_specs=[
            row(D_MODEL),
            pl.BlockSpec((1, ADA_CHUNKS, D_MODEL), lambda b, t: (b, 0, 0)),
            full(g), full(w), full(wf), full(fb), full(wuq), full(wukv), full(qg), full(kvg),
            row(LANES), row(LANES), full(cw),
        ],
        out_specs=[
            row(512), row(512), row(512), row(256), row(256), row(512),
            pl.BlockSpec((1, 8, rt), lambda b, t: (b, 0, t)),
            row(256), row(256), row(256), row(512),
        ],
        out_shape=[
            bf(512), bf(512), bf(512), bf(256), bf(256), bf(512),
            jax.ShapeDtypeStruct((bsz, 8, seq), F32),
            bf(256), bf(256), bf(256), bf(512),
        ],
        scratch_shapes=[pltpu.VMEM((8, GROUP_WIDTH), F32), pltpu.VMEM((16, LANES), F32)],
        compiler_params=_cparams(("arbitrary", "arbitrary")),
        name="norm_in_proj",
    )(x, mod, g, w, wf, fb, wuq, wukv, qg, kvg, cos_t, sin_t, cw)


def _tile_update(q, k_t, v_t, m_ref, acc_ref, bias=None, mask=None):
    s = _dot_nt(q, k_t)
    if bias is not None:
        s = s + bias
    if mask is not None:
        s = jnp.where(mask, s, NEG)
    m_prev = m_ref[...]
    m_new = jnp.maximum(m_prev, jnp.max(s, axis=-1, keepdims=True))
    alpha = jnp.exp2(m_prev - m_new)
    p = jnp.exp2(s - jnp.concatenate([m_new] * (s.shape[1] // LANES), axis=-1))
    acc_ref[...] = alpha * acc_ref[...] + _dot(p.astype(BF16), v_t)
    m_ref[...] = m_new


def _attn_kernel(*refs, kind, tq, tk):
    n_maps = 8 if kind == "D" else 4
    n_in = {"A": 3, "B": 4, "D": 6}[kind]
    q_ref, k_ref, v_ref = refs[0:3]
    o_ref = refs[n_in]
    q_scr = refs[n_in + 1]
    m_refs = refs[n_in + 2:n_in + 2 + n_maps]
    acc_refs = refs[n_in + 2 + n_maps:n_in + 2 + 2 * n_maps]
    if kind == "B":
        nf_ref = refs[3]
    elif kind == "D":
        bias_ref, lam_ref, sg_ref = refs[3:6]
    i = pl.program_id(1)
    ratio = tk // tq
    jd = i // ratio
    par = i % ratio
    off = par * tq
    lane = lax.broadcasted_iota(jnp.int32, (1, LANES), 1)
    low_half = lane < HEAD_DIM
    r_i = lax.broadcasted_iota(jnp.int32, (tq, tk), 0) + off
    c_i = lax.broadcasted_iota(jnp.int32, (tq, tk), 1)
    if kind == "B":
        diag_mask = c_i <= r_i
    else:
        shift = CHUNK.bit_length() - 1
        diag_mask = jnp.right_shift(c_i, shift) <= jnp.right_shift(r_i, shift)

    maps = []
    for head in range(4):
        slab = head // 2
        if kind == "A":
            q_scr[head] = q_ref[:, head * LANES:(head + 1) * LANES]
            maps.append((head, head, head))
        elif kind == "B":
            q = q_ref[:, slab * LANES:(slab + 1) * LANES]
            sel = low_half if head % 2 == 0 else jnp.logical_not(low_half)
            q_scr[head] = jnp.where(sel, q, jnp.zeros_like(q))
            maps.append((head, slab, head))
        else:
            q = q_ref[:, slab * LANES:(slab + 1) * LANES]
            for mp in range(2):
                lo = (2 * (head % 2) + mp) * DIFF_HALF
                sel = (lane >= lo) & (lane < lo + DIFF_HALF)
                q_scr[2 * head + mp] = jnp.where(sel, q, jnp.zeros_like(q))
                maps.append((2 * head + mp, slab, head))
    for idx in range(n_maps):
        m_refs[idx][...] = jnp.full(m_refs[idx].shape, NEG, F32)
        acc_refs[idx][...] = jnp.zeros(acc_refs[idx].shape, F32)

    def step(j, mode):
        start = pl.multiple_of(j * tk, tk)
        for idx, k_slab, head in maps:
            k_t = k_ref[pl.ds(start, tk), k_slab * LANES:(k_slab + 1) * LANES]
            v_t = v_ref[pl.ds(start, tk), head * LANES:(head + 1) * LANES]
            bias = None
            if kind == "B":
                bias = nf_ref[0, j][head:head + 1, :]
            elif kind == "D" and mode != "far":
                bias = bias_ref[head, par, 0 if mode == "diag" else 1]
            _tile_update(q_scr[idx], k_t, v_t, m_refs[idx], acc_refs[idx], bias=bias,
                         mask=diag_mask if mode == "diag" else None)

    def far_pair(jj, carry):
        step(2 * jj, "far")
        step(2 * jj + 1, "far")
        return carry

    n_far = jnp.maximum(jd - 1, 0)
    lax.fori_loop(0, n_far // 2, far_pair, 0)

    @pl.when(n_far % 2 == 1)
    def _():
        step(n_far - 1, "far")

    @pl.when(jd >= 1)
    def _():
        step(jd - 1, "prev")
        step(jd, "diag")

    @pl.when(jd == 0)
    def _():
        step(jd, "diag")

    def normalized(idx):
        acc = acc_refs[idx][...]
        return acc / pltpu.roll(acc, HEAD_DIM, axis=1)

    for slab in range(2):
        outs = []
        for sub in range(2):
            head = 2 * slab + sub
            if kind == "D":
                outs.append(normalized(2 * head) - lam_ref[...] * normalized(2 * head + 1))
            else:
                outs.append(normalized(head))
        o_slab = jnp.where(low_half, outs[0], outs[1])
        if kind == "D":
            sq = o_slab * o_slab
            s_lo = jnp.sum(jnp.where(low_half, sq, 0.0), axis=-1, keepdims=True)
            s_hi = jnp.sum(jnp.where(low_half, 0.0, sq), axis=-1, keepdims=True)
            ms = jnp.where(low_half, s_lo, s_hi) * (1.0 / HEAD_DIM)
            o_slab = o_slab * lax.rsqrt(ms + EPS) * sg_ref[...]
        o_ref[:, slab * LANES:(slab + 1) * LANES] = o_slab.astype(BF16)


def _attn_call(kind, q, k, v, extra, bsz, seq):
    n = bsz * seq
    tq = min(ATT_Q, seq)
    tk = min(ATT_K, seq)
    nq = seq // tq
    wq = q.shape[1]
    in_specs = [
        pl.BlockSpec((tq, wq), lambda b, i: (b * nq + i, 0)),
        pl.BlockSpec((seq, k.shape[1]), lambda b, i: (b, 0)),
        pl.BlockSpec((seq, v.shape[1]), lambda b, i: (b, 0)),
    ]
    if kind == "B":
        in_specs.append(pl.BlockSpec((1, seq // tk, 8, tk), lambda b, i: (b, 0, 0, 0)))
    elif kind == "D":
        bias, lam, sg = extra
        in_specs += [
            pl.BlockSpec(bias.shape, lambda b, i: (0,) * bias.ndim),
            pl.BlockSpec(lam.shape, lambda b, i: (0, 0)),
            pl.BlockSpec(sg.shape, lambda b, i: (0, 0)),
        ]
    n_maps = 8 if kind == "D" else 4
    return pl.pallas_call(
        functools.partial(_attn_kernel, kind=kind, tq=tq, tk=tk),
        grid=(bsz, nq),
        in_specs=in_specs,
        out_specs=pl.BlockSpec((tq, GROUP_WIDTH), lambda b, i: (b * nq + i, 0)),
        out_shape=jax.ShapeDtypeStruct((n, GROUP_WIDTH), BF16),
        scratch_shapes=(
            [pltpu.VMEM((n_maps, tq, LANES), BF16)]
            + [pltpu.VMEM((tq, LANES), F32) for _ in range(2 * n_maps)]
        ),
        compiler_params=_cparams(("arbitrary", "arbitrary")),
        name="attn_" + kind,
    )(q, k, v, *extra)


def _out_router_kernel(oa_ref, ob_ref, oc_ref, od_ref, wo_ref, x_ref, mod_ref, g_ref, wr_hi_ref, wr_lo_ref,
                       rb_ref, xo_ref, h2_ref, route_ref):
    gw = GROUP_WIDTH
    mix = (_dot(oa_ref[...], wo_ref[0:gw]) + _dot(ob_ref[...], wo_ref[gw:2 * gw])
           + _dot(oc_ref[...], wo_ref[2 * gw:3 * gw]) + _dot(od_ref[...], wo_ref[3 * gw:4 * gw]))
    mod = mod_ref[0]
    x1 = x_ref[...] + mod[2:3] * mix
    xo_ref[...] = x1
    hn = x1 * lax.rsqrt(jnp.mean(x1 * x1, axis=-1, keepdims=True) + EPS) * g_ref[...]
    h2 = hn * (1.0 + mod[4:5]) + mod[3:4]
    h2_ref[...] = h2

    a_hi, a_lo = _split_bf16(h2)
    w_hi = wr_hi_ref[...]
    lg = _dot_nt(w_hi, a_hi) + _dot_nt(w_hi, a_lo) + _dot_nt(wr_lo_ref[...], a_hi) + rb_ref[...]
    gl = lg[0:N_GROUPS]
    rows = gl.shape[1]
    iota8 = lax.broadcasted_iota(jnp.int32, (N_GROUPS, rows), 0)
    gmax = jnp.max(gl, axis=0, keepdims=True)
    grp = jnp.min(jnp.where(gl == gmax, iota8, N_GROUPS), axis=0, keepdims=True)
    p_grp = 1.0 / jnp.sum(jnp.exp(gl - gmax), axis=0, keepdims=True)
    esel = jnp.zeros((EXPERTS_PER_GROUP, rows), F32)
    for gi in range(N_GROUPS):
        lo = N_GROUPS + gi * EXPERTS_PER_GROUP
        esel = jnp.where(grp == gi, lg[lo:lo + EXPERTS_PER_GROUP], esel)
    ee = jnp.exp(esel - jnp.max(esel, axis=0, keepdims=True))
    ps = ee / jnp.sum(ee, axis=0, keepdims=True)
    p1 = jnp.max(ps, axis=0, keepdims=True)
    i1 = jnp.min(jnp.where(ps == p1, iota8, EXPERTS_PER_GROUP), axis=0, keepdims=True)
    ps2 = jnp.where(iota8 == i1, -1.0, ps)
    p2 = jnp.max(ps2, axis=0, keepdims=True)
    i2 = jnp.min(jnp.where(ps2 == p2, iota8, EXPERTS_PER_GROUP), axis=0, keepdims=True)
    den = p1 + p2
    g1 = p_grp * p1 / den
    g2 = p_grp * p2 / den
    e1 = (grp * EXPERTS_PER_GROUP + i1).astype(F32)
    e2 = (grp * EXPERTS_PER_GROUP + i2).astype(F32)
    zero = jnp.zeros_like(g1)
    route_ref[...] = jnp.concatenate([g1, g2, e1, e2, zero, zero, zero, zero], axis=0)


def _out_router_call(oa, ob, oc, od, wo, x, mod, g, wr_hi, wr_lo, rb, bsz, seq):
    n = bsz * seq
    rt = min(ROW_TILE, seq)
    nt = seq // rt
    row = lambda width: pl.BlockSpec((rt, width), lambda t: (t, 0))
    full = lambda a: pl.BlockSpec(a.shape, lambda t: (0,) * a.ndim)
    return pl.pallas_call(
        _out_router_kernel,
        grid=(n // rt,),
        in_specs=[
            row(GROUP_WIDTH), row(GROUP_WIDTH), row(GROUP_WIDTH), row(GROUP_WIDTH), full(wo),
            row(D_MODEL),
            pl.BlockSpec((1, ADA_CHUNKS, D_MODEL), lambda t: (t // nt, 0, 0)),
            full(g), full(wr_hi), full(wr_lo), full(rb),
        ],
        out_specs=[row(D_MODEL), row(D_MODEL), pl.BlockSpec((8, rt), lambda t: (0, t))],
        out_shape=[
            jax.ShapeDtypeStruct((n, D_MODEL), F32),
            jax.ShapeDtypeStruct((n, D_MODEL), F32),
            jax.ShapeDtypeStruct((8, n), F32),
        ],
        compiler_params=_cparams(("arbitrary",)),
        name="out_proj_router",
    )(oa, ob, oc, od, wo, x, mod, g, wr_hi, wr_lo, rb)


def _dispatch_kernel(np_ref, slot_ref, lp_ref, h_ref, init_hbm, xs_hbm, sorted_buf, sem):
    del init_hbm
    t = pl.program_id(0)
    rows_l = sorted_buf.shape[0]
    lp = lp_ref[0]
    pos = lax.broadcasted_iota(jnp.int32, (rows_l, lp.shape[1]), 0)
    hit = pos == lp[0:1, :]
    for k in range(1, TOP_K):
        hit = hit | (pos == lp[k:k + 1, :])
    perm = jnp.where(hit, 1.0, 0.0).astype(BF16)
    ordered = _dot(perm, h_ref[...].astype(BF16))
    sorted_buf[...] = _pack_bf16_pair(ordered[:, 0:D_MODEL // 2], ordered[:, D_MODEL // 2:D_MODEL])

    def piece(q):
        src = sorted_buf.at[pl.ds(pl.multiple_of(q * PIECE, PIECE), PIECE)]
        slot = pl.multiple_of(slot_ref[0, 0, q], PIECE)
        return pltpu.make_async_copy(src, xs_hbm.at[pl.ds(slot, PIECE)], sem)

    def issue(q, carry):
        piece(q).start()
        return carry

    def drain(q, carry):
        piece(q).wait()
        return carry
    lax.fori_loop(0, np_ref[t], issue, 0)
    lax.fori_loop(0, np_ref[t], drain, 0)


def _dispatch_call(n_pieces, piece_slot, lp_rows, h2, n_slots):
    n = h2.shape[0]
    n_tiles, _, q_max = piece_slot.shape
    td = n // n_tiles
    init = jnp.zeros((n_slots, D_MODEL // 2), jnp.int32)
    grid_spec = pltpu.PrefetchScalarGridSpec(
        num_scalar_prefetch=1,
        grid=(n_tiles,),
        in_specs=[
            pl.BlockSpec((1, 1, q_max), lambda t, npc: (t, 0, 0), memory_space=pltpu.SMEM),
            pl.BlockSpec((1, 8, td), lambda t, npc: (t, 0, 0)),
            pl.BlockSpec((td, D_MODEL), lambda t, npc: (t, 0)),
            pl.BlockSpec(memory_space=pl.ANY),
        ],
        out_specs=pl.BlockSpec(memory_space=pl.ANY),
        scratch_shapes=[pltpu.VMEM((q_max * PIECE, D_MODEL // 2), jnp.int32), pltpu.SemaphoreType.DMA],
    )
    return pl.pallas_call(
        _dispatch_kernel,
        grid_spec=grid_spec,
        out_shape=jax.ShapeDtypeStruct((n_slots, D_MODEL // 2), jnp.int32),
        input_output_aliases={4: 0},
        compiler_params=_cparams(("arbitrary",)),
        name="moe_dispatch",
    )(n_pieces, piece_slot, lp_rows, h2, init)


def _expert_kernel(be_ref, nact_ref, x_ref, wg_ref, wu_ref, wd_ref, y_ref, wgb, wub, wdb):
    i = pl.program_id(0)

    @pl.when(i < nact_ref[0])
    def _():
        prev = be_ref[jnp.maximum(i - 1, 0)]

        @pl.when((i == 0) | (be_ref[i] != prev))
        def _():
            wgb[...] = wg_ref[0, 0].astype(BF16)
            wub[...] = wu_ref[0, 0].astype(BF16)
            wdb[...] = wd_ref[0, 0].astype(BF16)

        half = D_MODEL // 2
        x_lo, x_hi = _unpack_bf16_pair(x_ref[...])
        gate = _dot(x_lo, wgb[0:half]) + _dot(x_hi, wgb[half:D_MODEL])
        up = _dot(x_lo, wub[0:half]) + _dot(x_hi, wub[half:D_MODEL])
        hid = gate * _sigmoid(gate) * up
        y = _dot(hid.astype(BF16), wdb[...])
        y_ref[...] = _pack_bf16_pair(y[:, 0:half], y[:, half:D_MODEL])

    @pl.when(i >= nact_ref[0])
    def _():
        y_ref[...] = jnp.zeros_like(y_ref)


def _expert_call(block_expert, n_active, x_slots, w_gate, w_up, w_down, layer):
    n_blocks = block_expert.shape[0]
    tm = MOE_TILE
    w_idx = lambda i, be, na: (layer, be[i], 0, 0)
    grid_spec = pltpu.PrefetchScalarGridSpec(
        num_scalar_prefetch=2,
        grid=(n_blocks,),
        in_specs=[
            pl.BlockSpec((tm, D_MODEL // 2), lambda i, be, na: (jnp.minimum(i, na[0] - 1), 0)),
            pl.BlockSpec((1, 1, D_MODEL, D_EXPERT), w_idx),
            pl.BlockSpec((1, 1, D_MODEL, D_EXPERT), w_idx),
            pl.BlockSpec((1, 1, D_EXPERT, D_MODEL), w_idx),
        ],
        out_specs=pl.BlockSpec((tm, D_MODEL // 2), lambda i, be, na: (i, 0)),
        scratch_shapes=[
            pltpu.VMEM((D_MODEL, D_EXPERT), BF16),
            pltpu.VMEM((D_MODEL, D_EXPERT), BF16),
            pltpu.VMEM((D_EXPERT, D_MODEL), BF16),
        ],
    )
    return pl.pallas_call(
        _expert_kernel,
        grid_spec=grid_spec,
        out_shape=jax.ShapeDtypeStruct((n_blocks * tm, D_MODEL // 2), jnp.int32),
        compiler_params=_cparams(("arbitrary",)),
        name="expert_mlp",
    )(block_expert, n_active, x_slots, w_gate, w_up, w_down)


def _combine_kernel(slot_ref, y_hbm, x_ref, route_ref, mod_ref, fg_ref, o_ref, ybuf, sem, *, final):
    n_q = slot_ref.shape[2]

    def issue(q, carry):
        dst = ybuf.at[pl.ds(pl.multiple_of(q * PIECE, PIECE), PIECE)]
        slot = pl.multiple_of(slot_ref[0, 0, q], PIECE)
        pltpu.make_async_copy(y_hbm.at[pl.ds(slot, PIECE)], dst, sem).start()
        return carry
    lax.fori_loop(0, n_q, issue, 0)
    pltpu.make_async_copy(y_hbm.at[pl.ds(0, ybuf.shape[0])], ybuf, sem).wait()

    rt = route_ref[...]
    pos = lax.broadcasted_iota(jnp.int32, (rt.shape[0], ybuf.shape[0]), 1)
    weights = jnp.zeros(pos.shape, F32)
    for k in range(TOP_K):
        lp_k = rt[:, 4 + k:5 + k].astype(jnp.int32)
        weights = jnp.where(pos == lp_k, rt[:, k:k + 1], weights)
    y_lo, y_hi = _unpack_bf16_pair(ybuf[...])
    wb = weights.astype(BF16)
    y = jnp.concatenate([_dot(wb, y_lo), _dot(wb, y_hi)], axis=-1)
    x2 = x_ref[...] + mod_ref[0][5:6] * y
    if final:
        x2 = x2 * lax.rsqrt(jnp.mean(x2 * x2, axis=-1, keepdims=True) + EPS) * fg_ref[...]
    o_ref[...] = x2


def _combine_call(piece_slot, y_slots, x, route_lp, mod, fg, bsz, seq, final):
    n = bsz * seq
    n_tiles, _, q_max = piece_slot.shape
    tc = n // n_tiles
    nt = seq // tc
    return pl.pallas_call(
        functools.partial(_combine_kernel, final=final),
        grid=(n_tiles,),
        in_specs=[
            pl.BlockSpec((1, 1, q_max), lambda t: (t, 0, 0), memory_space=pltpu.SMEM),
            pl.BlockSpec(memory_space=pl.ANY),
            pl.BlockSpec((tc, D_MODEL), lambda t: (t, 0)),
            pl.BlockSpec((tc, 8), lambda t: (t, 0)),
            pl.BlockSpec((1, ADA_CHUNKS, D_MODEL), lambda t: (t // nt, 0, 0)),
            pl.BlockSpec((1, D_MODEL), lambda t: (0, 0)),
        ],
        out_specs=pl.BlockSpec((tc, D_MODEL), lambda t: (t, 0)),
        out_shape=jax.ShapeDtypeStruct((n, D_MODEL), F32),
        scratch_shapes=[pltpu.VMEM((q_max * PIECE, D_MODEL // 2), jnp.int32), pltpu.SemaphoreType.DMA],
        compiler_params=_cparams(("arbitrary",)),
        name="moe_combine",
    )(piece_slot, y_slots, x, route_lp, mod, fg)


def _t5_bucket(rel):
    nb = T5_BUCKETS // 2
    max_exact = nb // 2
    bucket = jnp.where(rel > 0, nb, 0)
    n = jnp.abs(rel)
    large = max_exact + (jnp.log(jnp.maximum(n, 1).astype(F32) / max_exact)
                         / math.log(T5_MAX_DIST / max_exact) * (nb - max_exact)).astype(jnp.int32)
    large = jnp.minimum(large, nb - 1)
    return bucket + jnp.where(n < max_exact, n, large)


def _t5_tiles(t5_table, tq, tk):
    assert tk >= T5_MAX_DIST and tk % tq == 0
    par = jnp.arange(tk // tq, dtype=jnp.int32)[:, None, None, None]
    back = jnp.arange(2, dtype=jnp.int32)[None, :, None, None]
    r = jnp.arange(tq, dtype=jnp.int32)[None, None, :, None]
    c = jnp.arange(tk, dtype=jnp.int32)[None, None, None, :]
    rel = c - back * tk - (r + par * tq)
    far = t5_table[_t5_bucket(jnp.array(-4 * T5_MAX_DIST, jnp.int32))]
    bucket = _t5_bucket(rel)[..., None]
    b = jnp.zeros(rel.shape + (t5_table.shape[1],), F32)
    for k in range(T5_BUCKETS):
        b = jnp.where(bucket == k, t5_table[k], b)
    b = b - far
    return (b.transpose(4, 0, 1, 2, 3) * LOG2E).astype(F32)


def _place(cols, width, offset):
    z = jnp.zeros((cols.shape[0], width), cols.dtype)
    return z.at[:, offset:offset + cols.shape[1]].set(cols)


def _rot_half(cols):
    half = cols.shape[1] // 2
    return jnp.concatenate([-cols[:, half:], cols[:, :half]], axis=1)


def _prep_in_proj(w_in):
    cq = _place(w_in[:, 0:MLA_Q_LORA], _CQ_PAD, 0)
    ckv = w_in[:, MLA_Q_LORA:MLA_Q_LORA + MLA_KV_LORA]
    kr = w_in[:, MLA_Q_LORA + MLA_KV_LORA:_O_FOX]
    kr_p = _place(kr, LANES, MLA_NOPE)
    kr_rot = _place(_rot_half(kr), LANES, MLA_NOPE)
    fox = w_in[:, _O_FOX:_O_FOX + 3 * GROUP_WIDTH]
    wf = w_in[:, _O_FOX + 3 * GROUP_WIDTH:_O_CONV]
    conv = w_in[:, _O_CONV:_O_DIFF]
    diff = w_in[:, _O_DIFF:_O_DIFF + 3 * GROUP_WIDTH]
    w = jnp.concatenate([cq, ckv, kr_p, kr_rot, fox, conv, diff], axis=1).astype(BF16)
    wf_t = _place(wf, 16, 0).T.astype(BF16)
    return w, wf_t


def _prep_mla(w_uq, w_ukv):
    hd = MLA_NOPE + MLA_ROPE
    plain, rot, k_nope, v = [], [], [], []
    for h in range(4):
        wh = w_uq[:, h * hd:(h + 1) * hd]
        plain.append(_place(wh, LANES, 0))
        rot.append(_place(_rot_half(wh[:, MLA_NOPE:]), LANES, MLA_NOPE))
        kv = w_ukv[:, h * 2 * HEAD_DIM:(h + 1) * 2 * HEAD_DIM]
        k_nope.append(_place(kv[:, :MLA_NOPE], LANES, 0))
        v.append(kv[:, MLA_NOPE:])
    wuq = jnp.concatenate(plain + rot, axis=1)
    wuq = jnp.concatenate([wuq, jnp.zeros((_CQ_PAD - MLA_Q_LORA, wuq.shape[1]), wuq.dtype)], axis=0)
    wukv = jnp.concatenate(k_nope + v, axis=1)
    return wuq.astype(BF16), wukv.astype(BF16)


def _rope_tables(positions):
    half = MLA_ROPE // 2
    inv_freq = ROPE_THETA ** (-jnp.arange(half, dtype=F32) / half)
    ang = positions.astype(F32).reshape(-1)[:, None] * inv_freq
    cos, sin = jnp.cos(ang), jnp.sin(ang)
    n = ang.shape[0]
    cos_t = jnp.ones((n, LANES), F32).at[:, MLA_NOPE:MLA_NOPE + MLA_ROPE].set(jnp.concatenate([cos, cos], 1))
    sin_t = jnp.zeros((n, LANES), F32).at[:, MLA_NOPE:MLA_NOPE + MLA_ROPE].set(jnp.concatenate([sin, sin], 1))
    return cos_t, sin_t


def _piece_rows(tile):
    rows = tile * TOP_K + N_EXPERTS * (PIECE - 1)
    return -(-rows // LANES) * LANES


def _dispatch(expert, n_tok, tile):
    tm = MOE_TILE
    n_assign = n_tok * TOP_K
    per_tile = tile * TOP_K
    n_tiles = n_tok // tile
    q_max = _piece_rows(tile) // PIECE
    chunk = LANES
    experts = jnp.arange(N_EXPERTS, dtype=jnp.int32)
    e_flat = expert.reshape(n_assign)
    onehot = e_flat[:, None] == experts[None, :]
    oh = onehot.astype(BF16).reshape(n_assign // chunk, chunk, N_EXPERTS)
    tri = (jnp.arange(chunk)[:, None] >= jnp.arange(chunk)[None, :]).astype(BF16)
    within = jnp.einsum("ij,cjk->cik", tri, oh, preferred_element_type=F32)
    total = within[:, chunk - 1, :]
    before = jnp.cumsum(total, axis=0) - total
    running = (within + before[:, None, :]).reshape(n_tiles, per_tile, N_EXPERTS)
    onehot = onehot.reshape(n_tiles, per_tile, N_EXPERTS)

    run_end = running[:, per_tile - 1, :]
    run_start = jnp.concatenate([jnp.zeros((1, N_EXPERTS), F32), run_end[:-1]], axis=0)
    pieces = ((run_end - run_start).astype(jnp.int32) + PIECE - 1) // PIECE
    run_rows = pieces * PIECE
    rows_before = jnp.cumsum(run_rows, axis=0) - run_rows
    counts = rows_before[-1] + run_rows[-1]
    padded = (counts + tm - 1) // tm * tm
    pad_ends = jnp.cumsum(padded)
    pad_starts = pad_ends - padded

    piece_end = jnp.cumsum(pieces, axis=1)
    piece_first = piece_end - pieces
    local_start = (piece_first * PIECE).astype(F32)
    local = running - 1.0 - run_start[:, None, :] + local_start[:, None, :]
    lp = jnp.sum(jnp.where(onehot, local, 0.0), axis=2).astype(jnp.int32)
    lp = lp.reshape(n_tok, TOP_K)

    q = jnp.arange(q_max, dtype=jnp.int32)[None, :]
    e_of_q = jnp.sum((piece_end[:, None, :] <= q[:, :, None]).astype(jnp.int32), axis=2)
    pick = e_of_q[:, :, None] == experts[None, None, :]
    run_slot = pad_starts[None, :] + rows_before
    first_q = jnp.sum(jnp.where(pick, piece_first[:, None, :], 0), axis=2)
    slot_q = jnp.sum(jnp.where(pick, run_slot[:, None, :], 0), axis=2)
    piece_slot = jnp.where(e_of_q < N_EXPERTS, slot_q + (q - first_q) * PIECE, 0)
    n_pieces = piece_end[:, N_EXPERTS - 1]

    n_blocks = -(-(n_assign + N_EXPERTS * (n_tiles * (PIECE - 1) + tm - 1)) // tm)
    block_start = jnp.arange(n_blocks, dtype=jnp.int32) * tm
    block_expert = jnp.minimum(
        jnp.sum((pad_ends[None, :] <= block_start[:, None]).astype(jnp.int32), axis=1), N_EXPERTS - 1)
    n_active = (pad_ends[-1] // tm).astype(jnp.int32).reshape(1)
    return (block_expert, n_active, lp, piece_slot.reshape(n_tiles, 1, q_max).astype(jnp.int32),
            n_pieces.astype(jnp.int32), n_blocks * tm)


def kernel(x, c, positions, t5_table, ada_w, ada_b, norm_mix_g, norm_ffn_g, w_in, mla_q_norm_g, mla_w_uq, mla_kv_norm_g, mla_w_ukv, fox_forget_b, conv_w, diff_lambda, diff_subln_g, w_out, router_group_w, router_group_b, router_expert_w, router_expert_b, expert_w_gate, expert_w_up, expert_w_down, final_norm_g):
    bsz, seq, d = x.shape
    n = bsz * seq
    tq = min(ATT_Q, seq)
    tk = min(ATT_K, seq)
    tc = min(CMB_TILE, seq)

    mods = _ada_call(c, ada_w, ada_b).reshape(DEPTH, bsz, ADA_CHUNKS, D_MODEL)
    cos_t, sin_t = _rope_tables(positions)
    bias_tiles = _t5_tiles(t5_table, tq, tk)
    xf = x.reshape(n, d)

    for layer in range(DEPTH):
        mod = mods[layer]
        w, wf_t = _prep_in_proj(w_in[layer])
        wuq, wukv = _prep_mla(mla_w_uq[layer], mla_w_ukv[layer])
        fb = _place(fox_forget_b[layer][None, :], 16, 0).T.astype(F32)
        qg = _place(mla_q_norm_g[layer][None, :], _CQ_PAD, 0)
        kvg = mla_kv_norm_g[layer][None, :]
        cw = _place(conv_w[layer].T, 8, 0).T
        (qa, ka, va, qb, kb, vb, nf, oc, qd, kd, vd) = _proj_call(
            xf, mod, norm_mix_g[layer][None, :], w, wf_t, fb, wuq, wukv, qg, kvg, cos_t, sin_t, cw, bsz, seq)

        oa = _attn_call("A", qa, ka, va, (), bsz, seq)
        nf_t = nf.reshape(bsz, 8, seq // tk, tk).transpose(0, 2, 1, 3)
        ob = _attn_call("B", qb, kb, vb, (nf_t,), bsz, seq)
        lam_init = 0.8 - 0.6 * math.exp(-0.3 * layer)
        lp = diff_lambda[layer].astype(F32)
        lam = jnp.exp(jnp.sum(lp[0] * lp[1])) - jnp.exp(jnp.sum(lp[2] * lp[3])) + lam_init
        lam_row = jnp.full((1, LANES), lam, F32)
        sg = (jnp.concatenate([diff_subln_g[layer]] * 2) * (1.0 - lam_init))[None, :].astype(F32)
        od = _attn_call("D", qd, kd, vd, (bias_tiles, lam_row, sg), bsz, seq)

        wr = jnp.concatenate([router_group_w[layer], router_expert_w[layer]], axis=1)
        wr_t = _place(wr, 80, 0).T
        wr_hi = wr_t.astype(BF16)
        wr_lo = (wr_t - wr_hi.astype(F32)).astype(BF16)
        rb = _place(jnp.concatenate([router_group_b[layer], router_expert_b[layer]])[None, :], 80, 0).T
        x1, h2, route = _out_router_call(
            oa, ob, oc, od, w_out[layer].astype(BF16), xf, mod, norm_ffn_g[layer][None, :],
            wr_hi, wr_lo, rb.astype(F32), bsz, seq)

        route_t = route.T
        expert = route_t[:, 2:4].astype(jnp.int32)
        block_expert, n_active, lp, piece_slot, n_pieces, n_slots = _dispatch(expert, n, tc)
        lp_rows = jnp.full((n // tc, 8, tc), -1, jnp.int32).at[:, 0:TOP_K, :].set(
            lp.reshape(n // tc, tc, TOP_K).transpose(0, 2, 1))
        x_slots = _dispatch_call(n_pieces, piece_slot, lp_rows, h2, n_slots)
        y_slots = _expert_call(block_expert, n_active, x_slots,
                               expert_w_gate, expert_w_up, expert_w_down, layer)
        route_lp = jnp.concatenate([route_t[:, 0:4], lp.astype(F32), route_t[:, 6:8]], axis=1)
        xf = _combine_call(piece_slot, y_slots, x1, route_lp, mod, final_norm_g[None, :], bsz, seq,
                           final=(layer == DEPTH - 1))
    return xf.reshape(bsz, seq, d)
```

```python
import functools
import math

import jax
import jax.numpy as jnp
from jax import lax
from jax.experimental import pallas as pl
from jax.experimental.pallas import tpu as pltpu

F32 = jnp.float32
BF16 = jnp.bfloat16

D_MODEL = 1024
DEPTH = 4
CHUNK = 64
EPS = 1e-6
HEAD_DIM = 64
GROUP_WIDTH = 256
MLA_NOPE = 64
MLA_ROPE = 32
MLA_Q_LORA = 192
MLA_KV_LORA = 128
ROPE_THETA = 10000.0
DIFF_HALF = 32
T5_BUCKETS = 32
T5_MAX_DIST = 128
N_GROUPS = 8
EXPERTS_PER_GROUP = 8
N_EXPERTS = 64
TOP_K = 2
D_EXPERT = 512
ADA_CHUNKS = 6

LANES = 128
LOG2E = 1.4426950408889634
NEG = -1e30

_O_MLA = 0
_O_FOX = MLA_Q_LORA + MLA_KV_LORA + MLA_ROPE
_O_CONV = _O_FOX + 3 * GROUP_WIDTH + 4
_O_DIFF = _O_CONV + 3 * GROUP_WIDTH

_CQ_PAD = 256
_W_A = _CQ_PAD + MLA_KV_LORA + 2 * LANES
_W_B = 3 * GROUP_WIDTH
_W_C = 3 * GROUP_WIDTH
_W_D = 3 * GROUP_WIDTH
_P_TOTAL = _W_A + _W_B + _W_C + _W_D

ROW_TILE = 512
ATT_Q = 256
ATT_K = 256
MOE_TILE = 256
CMB_TILE = 512
PIECE = 8
VMEM_LIMIT = 56 * 1024 * 1024


def _cparams(sem):
    return pltpu.CompilerParams(dimension_semantics=sem, vmem_limit_bytes=VMEM_LIMIT)


def _split_bf16(a):
    hi = a.astype(BF16)
    lo = (a - hi.astype(F32)).astype(BF16)
    return hi, lo


def _dot(a, b):
    return jnp.dot(a, b, preferred_element_type=F32)


def _dot_nt(a, b):
    return lax.dot_general(a, b, (((1,), (1,)), ((), ())), preferred_element_type=F32)


def _sigmoid(z):
    return 1.0 / (1.0 + jnp.exp(-z))


_HIGH16 = -65536


def _pack_bf16_pair(lo, hi):
    lo_bits = pltpu.bitcast(lo.astype(BF16).astype(F32), jnp.int32)
    hi_bits = pltpu.bitcast(hi.astype(BF16).astype(F32), jnp.int32)
    sixteen = jnp.full(lo_bits.shape, 16, jnp.int32)
    return (hi_bits & _HIGH16) | lax.shift_right_logical(lo_bits, sixteen)


def _unpack_bf16_pair(packed):
    lo = pltpu.bitcast(lax.shift_left(packed, jnp.full(packed.shape, 16, jnp.int32)), F32)
    hi = pltpu.bitcast(packed & _HIGH16, F32)
    return lo.astype(BF16), hi.astype(BF16)


def _ada_kernel(c_ref, w_ref, b_ref, o_ref):
    c = c_ref[...]
    cond = c * _sigmoid(c)
    c_hi, c_lo = _split_bf16(cond)
    w_hi, w_lo = _split_bf16(w_ref[0])
    o_ref[0] = _dot(c_hi, w_hi) + _dot(c_hi, w_lo) + _dot(c_lo, w_hi) + b_ref[0]


def _ada_call(c, ada_w, ada_b):
    bsz = c.shape[0]
    n_col = ADA_CHUNKS * D_MODEL // D_MODEL
    return pl.pallas_call(
        _ada_kernel,
        grid=(DEPTH, n_col),
        in_specs=[
            pl.BlockSpec((bsz, D_MODEL), lambda l, j: (0, 0)),
            pl.BlockSpec((1, D_MODEL, D_MODEL), lambda l, j: (l, 0, j)),
            pl.BlockSpec((1, 1, D_MODEL), lambda l, j: (l, 0, j)),
        ],
        out_specs=pl.BlockSpec((1, bsz, D_MODEL), lambda l, j: (l, 0, j)),
        out_shape=jax.ShapeDtypeStruct((DEPTH, bsz, ADA_CHUNKS * D_MODEL), F32),
        compiler_params=_cparams(("arbitrary", "arbitrary")),
        name="ada_mod",
    )(c, ada_w, ada_b.reshape(DEPTH, 1, ADA_CHUNKS * D_MODEL))


def _values_with_ones(v):
    lane = lax.broadcasted_iota(jnp.int32, (1, LANES), 1)
    low_half = lane < HEAD_DIM
    slabs = []
    for head in range(4):
        pair = v[:, (head // 2) * LANES:(head // 2 + 1) * LANES]
        keep = low_half if head % 2 == 0 else jnp.logical_not(low_half)
        slabs.append(jnp.where(keep, pair, 1.0))
    return jnp.concatenate(slabs, axis=-1).astype(BF16)


def _proj_kernel(x_ref, mod_ref, g_ref, w_ref, wf_ref, fb_ref, wuq_ref, wukv_ref, qg_ref, kvg_ref,
                 cos_ref, sin_ref, cw_ref,
                 qa_ref, ka_ref, va_ref, qb_ref, kb_ref, vb_ref, nf_ref, oc_ref, qd_ref, kd_ref, vd_ref,
                 zc_ref, fc_ref, *, scale_a, scale_b, scale_d):
    t = pl.program_id(1)
    rows = x_ref.shape[0]

    @pl.when(t == 0)
    def _():
        zc_ref[...] = jnp.zeros_like(zc_ref)
        fc_ref[...] = jnp.zeros_like(fc_ref)

    x = x_ref[...]
    mod = mod_ref[0]
    hn = x * lax.rsqrt(jnp.mean(x * x, axis=-1, keepdims=True) + EPS) * g_ref[...]
    h = hn * (1.0 + mod[1:2]) + mod[0:1]
    hb = h.astype(BF16)

    z = _dot_nt(wf_ref[...], hb) + fb_ref[...]
    log_f = jnp.minimum(z, 0.0) - jnp.log(1.0 + jnp.exp(-jnp.abs(z)))
    r_i = lax.broadcasted_iota(jnp.int32, (rows, rows), 0)
    c_i = lax.broadcasted_iota(jnp.int32, (rows, rows), 1)
    tri = jnp.where(r_i <= c_i, 1.0, 0.0).astype(BF16)
    f_hi = log_f.astype(BF16)
    rem = log_f - f_hi.astype(F32)
    f_mid = rem.astype(BF16)
    f_lo = (rem - f_mid.astype(F32)).astype(BF16)
    cum = _dot(f_hi, tri) + _dot(f_mid, tri) + _dot(f_lo, tri) + fc_ref[:, 0:1]
    fc_ref[...] = jnp.broadcast_to(cum[:, rows - 1:rows], fc_ref.shape)
    nf_ref[0] = cum[0:8] * (-LOG2E)

    pa = _dot(hb, w_ref[:, 0:_W_A])
    cq = pa[:, 0:_CQ_PAD]
    ckv = pa[:, _CQ_PAD:_CQ_PAD + MLA_KV_LORA]
    kr = pa[:, _CQ_PAD + MLA_KV_LORA:_CQ_PAD + MLA_KV_LORA + LANES]
    krr = pa[:, _CQ_PAD + MLA_KV_LORA + LANES:_W_A]
    cqn = cq * lax.rsqrt(jnp.sum(cq * cq, axis=-1, keepdims=True) * (1.0 / MLA_Q_LORA) + EPS) * qg_ref[...]
    ckvn = ckv * lax.rsqrt(jnp.mean(ckv * ckv, axis=-1, keepdims=True) + EPS) * kvg_ref[...]
    q2 = _dot(cqn.astype(BF16), wuq_ref[...])
    kv2 = _dot(ckvn.astype(BF16), wukv_ref[...])
    cos = cos_ref[...]
    sin = sin_ref[...]
    cos4 = jnp.concatenate([cos] * 4, axis=-1)
    sin4 = jnp.concatenate([sin] * 4, axis=-1)
    qa = (q2[:, 0:512] * cos4 + q2[:, 512:1024] * sin4) * (scale_a * LOG2E)
    k_rope = kr * cos + krr * sin
    ka = kv2[:, 0:512] + jnp.concatenate([k_rope] * 4, axis=-1)
    qa_ref[...] = qa.astype(BF16)
    ka_ref[...] = ka.astype(BF16)
    va_ref[...] = _values_with_ones(kv2[:, 512:768])

    pb = _dot(hb, w_ref[:, _W_A:_W_A + _W_B])
    qb_ref[...] = (pb[:, 0:256] * (scale_b * LOG2E)).astype(BF16)
    kb_ref[...] = pb[:, 256:512].astype(BF16)
    vb_ref[...] = _values_with_ones(pb[:, 512:768])

    pc = _dot(hb, w_ref[:, _W_A + _W_B:_W_A + _W_B + _W_C])
    zz = pc[:, 256:512] * pc[:, 512:768]
    ext = jnp.concatenate([zc_ref[...], zz], axis=0)
    cw = cw_ref[...]
    conv = zz * cw[2:3] + ext[7:rows + 7] * cw[1:2] + ext[6:rows + 6] * cw[0:1]
    zc_ref[...] = zz[rows - 8:rows]
    oc_ref[...] = (pc[:, 0:256] * conv).astype(BF16)

    pd = _dot(hb, w_ref[:, _W_A + _W_B + _W_C:_P_TOTAL])
    qd_ref[...] = (pd[:, 0:256] * (scale_d * LOG2E)).astype(BF16)
    kd_ref[...] = pd[:, 256:512].astype(BF16)
    vd_ref[...] = _values_with_ones(pd[:, 512:768])


def _proj_call(x, mod, g, w, wf, fb, wuq, wukv, qg, kvg, cos_t, sin_t, cw, bsz, seq):
    n = bsz * seq
    rt = min(ROW_TILE, seq)
    nt = seq // rt
    row = lambda width: pl.BlockSpec((rt, width), lambda b, t: (b * nt + t, 0))
    full = lambda a: pl.BlockSpec(a.shape, lambda b, t: (0,) * a.ndim)
    bf = lambda width: jax.ShapeDtypeStruct((n, width), BF16)
    kern = functools.partial(
        _proj_kernel,
        scale_a=(MLA_NOPE + MLA_ROPE) ** -0.5, scale_b=HEAD_DIM ** -0.5, scale_d=DIFF_HALF ** -0.5)
    return pl.pallas_call(
        kern,
        grid=(bsz, nt),
        in_specs=[
            row(D_MODEL),
            pl.BlockSpec((1, ADA_CHUNKS, D_MODEL), lambda b, t: (b, 0, 0)),
            full(g), full(w), full(wf), full(fb), full(wuq), full(wukv), full(qg), full(kvg),
            row(LANES), row(LANES), full(cw),
        ],
        out_specs=[
            row(512), row(512), row(512), row(256), row(256), row(512),
            pl.BlockSpec((1, 8, rt), lambda b, t: (b, 0, t)),
            row(256), row(256), row(256), row(512),
        ],
        out_shape=[
            bf(512), bf(512), bf(512), bf(256), bf(256), bf(512),
            jax.ShapeDtypeStruct((bsz, 8, seq), F32),
            bf(256), bf(256), bf(256), bf(512),
        ],
        scratch_shapes=[pltpu.VMEM((8, GROUP_WIDTH), F32), pltpu.VMEM((16, LANES), F32)],
        compiler_params=_cparams(("arbitrary", "arbitrary")),
        name="norm_in_proj",
    )(x, mod, g, w, wf, fb, wuq, wukv, qg, kvg, cos_t, sin_t, cw)


def _tile_update(q, k_t, v_t, m_ref, acc_ref, bias=None, mask=None):
    s = _dot_nt(q, k_t)
    if bias is not None:
        s = s + bias
    if mask is not None:
        s = jnp.where(mask, s, NEG)
    m_prev = m_ref[...]
    m_new = jnp.maximum(m_prev, jnp.max(s, axis=-1, keepdims=True))
    alpha = jnp.exp2(m_prev - m_new)
    p = jnp.exp2(s - jnp.concatenate([m_new] * (s.shape[1] // LANES), axis=-1))
    acc_ref[...] = alpha * acc_ref[...] + _dot(p.astype(BF16), v_t)
    m_ref[...] = m_new


_ATTN_INPUTS = {"A": 3, "B": 4, "D": 6}
_ATTN_MAPS = {"A": 4, "B": 4, "D": 8}


def _attn_kernel(*refs, kinds, tq, tk):
    pos = 0
    ins, outs, q_scrs, m_refs, acc_refs = {}, {}, {}, {}, {}
    for kd in kinds:
        ins[kd] = refs[pos:pos + _ATTN_INPUTS[kd]]
        pos += _ATTN_INPUTS[kd]
    for kd in kinds:
        outs[kd] = refs[pos]
        pos += 1
    for kd in kinds:
        nm = _ATTN_MAPS[kd]
        q_scrs[kd] = refs[pos]
        m_refs[kd] = refs[pos + 1:pos + 1 + nm]
        acc_refs[kd] = refs[pos + 1 + nm:pos + 1 + 2 * nm]
        pos += 1 + 2 * nm
    i = pl.program_id(1)
    ratio = tk // tq
    jd = i // ratio
    par = i % ratio
    off = par * tq
    lane = lax.broadcasted_iota(jnp.int32, (1, LANES), 1)
    low_half = lane < HEAD_DIM
    r_i = lax.broadcasted_iota(jnp.int32, (tq, tk), 0) + off
    c_i = lax.broadcasted_iota(jnp.int32, (tq, tk), 1)
    shift = CHUNK.bit_length() - 1
    diag_masks = {
        "frame": c_i <= r_i,
        "chunk": jnp.right_shift(c_i, shift) <= jnp.right_shift(r_i, shift),
    }

    maps = []
    for kd in kinds:
        q_ref, q_scr = ins[kd][0], q_scrs[kd]
        for head in range(4):
            slab = head // 2
            if kd == "A":
                q_scr[head] = q_ref[:, head * LANES:(head + 1) * LANES]
                maps.append((kd, head, head, head))
            elif kd == "B":
                q = q_ref[:, slab * LANES:(slab + 1) * LANES]
                sel = low_half if head % 2 == 0 else jnp.logical_not(low_half)
                q_scr[head] = jnp.where(sel, q, jnp.zeros_like(q))
                maps.append((kd, head, slab, head))
            else:
                q = q_ref[:, slab * LANES:(slab + 1) * LANES]
                for mp in range(2):
                    lo = (2 * (head % 2) + mp) * DIFF_HALF
                    sel = (lane >= lo) & (lane < lo + DIFF_HALF)
                    q_scr[2 * head + mp] = jnp.where(sel, q, jnp.zeros_like(q))
                    maps.append((kd, 2 * head + mp, slab, head))
        for idx in range(_ATTN_MAPS[kd]):
            m_refs[kd][idx][...] = jnp.full(m_refs[kd][idx].shape, NEG, F32)
            acc_refs[kd][idx][...] = jnp.zeros(acc_refs[kd][idx].shape, F32)

    def step(j, mode):
        start = pl.multiple_of(j * tk, tk)
        for kd, idx, k_slab, head in maps:
            k_t = ins[kd][1][pl.ds(start, tk), k_slab * LANES:(k_slab + 1) * LANES]
            v_t = ins[kd][2][pl.ds(start, tk), head * LANES:(head + 1) * LANES]
            bias = None
            if kd == "B":
                bias = ins[kd][3][0, j][head:head + 1, :]
            elif kd == "D" and mode != "far":
                bias = ins[kd][3][head, par, 0 if mode == "diag" else 1]
            mask = diag_masks["frame" if kd == "B" else "chunk"] if mode == "diag" else None
            _tile_update(q_scrs[kd][idx], k_t, v_t, m_refs[kd][idx], acc_refs[kd][idx],
                         bias=bias, mask=mask)

    def far_pair(jj, carry):
        step(2 * jj, "far")
        step(2 * jj + 1, "far")
        return carry

    n_far = jnp.maximum(jd - 1, 0)
    lax.fori_loop(0, n_far // 2, far_pair, 0)

    @pl.when(n_far % 2 == 1)
    def _():
        step(n_far - 1, "far")

    @pl.when(jd >= 1)
    def _():
        step(jd - 1, "prev")
        step(jd, "diag")

    @pl.when(jd == 0)
    def _():
        step(jd, "diag")

    def normalized(kd, idx):
        acc = acc_refs[kd][idx][...]
        return acc / pltpu.roll(acc, HEAD_DIM, axis=1)

    for kd in kinds:
        for slab in range(2):
            heads = []
            for sub in range(2):
                head = 2 * slab + sub
                if kd == "D":
                    lam = ins[kd][4][...]
                    heads.append(normalized(kd, 2 * head) - lam * normalized(kd, 2 * head + 1))
                else:
                    heads.append(normalized(kd, head))
            o_slab = jnp.where(low_half, heads[0], heads[1])
            if kd == "D":
                sq = o_slab * o_slab
                s_lo = jnp.sum(jnp.where(low_half, sq, 0.0), axis=-1, keepdims=True)
                s_hi = jnp.sum(jnp.where(low_half, 0.0, sq), axis=-1, keepdims=True)
                ms = jnp.where(low_half, s_lo, s_hi) * (1.0 / HEAD_DIM)
                o_slab = o_slab * lax.rsqrt(ms + EPS) * ins[kd][5][...]
            outs[kd][:, slab * LANES:(slab + 1) * LANES] = o_slab.astype(BF16)


def _attn_call(groups, bsz, seq):
    n = bsz * seq
    tq = min(ATT_Q, seq)
    tk = min(ATT_K, seq)
    nq = seq // tq
    kinds = tuple(kd for kd, _ in groups)
    operands, in_specs, scratch = [], [], []
    for kd, (q, k, v, *extra) in groups:
        operands += [q, k, v, *extra]
        in_specs += [
            pl.BlockSpec((tq, q.shape[1]), lambda b, i: (b * nq + i, 0)),
            pl.BlockSpec((seq, k.shape[1]), lambda b, i: (b, 0)),
            pl.BlockSpec((seq, v.shape[1]), lambda b, i: (b, 0)),
        ]
        if kd == "B":
            in_specs.append(pl.BlockSpec((1, seq // tk, 8, tk), lambda b, i: (b, 0, 0, 0)))
        else:
            in_specs += [pl.BlockSpec(e.shape, lambda b, i, nd=e.ndim: (0,) * nd) for e in extra]
        scratch += [pltpu.VMEM((_ATTN_MAPS[kd], tq, LANES), BF16)]
        scratch += [pltpu.VMEM((tq, LANES), F32) for _ in range(2 * _ATTN_MAPS[kd])]
    out_spec = pl.BlockSpec((tq, GROUP_WIDTH), lambda b, i: (b * nq + i, 0))
    return pl.pallas_call(
        functools.partial(_attn_kernel, kinds=kinds, tq=tq, tk=tk),
        grid=(bsz, nq),
        in_specs=in_specs,
        out_specs=[out_spec] * len(kinds),
        out_shape=[jax.ShapeDtypeStruct((n, GROUP_WIDTH), BF16)] * len(kinds),
        scratch_shapes=scratch,
        compiler_params=_cparams(("arbitrary", "arbitrary")),
        name="attn_" + "".join(kinds),
    )(*operands)


def _out_router_kernel(oa_ref, ob_ref, oc_ref, od_ref, wo_ref, x_ref, mod_ref, g_ref, wr_hi_ref, wr_lo_ref,
                       rb_ref, xo_ref, h2_ref, route_ref):
    gw = GROUP_WIDTH
    mix = (_dot(oa_ref[...], wo_ref[0:gw]) + _dot(ob_ref[...], wo_ref[gw:2 * gw])
           + _dot(oc_ref[...], wo_ref[2 * gw:3 * gw]) + _dot(od_ref[...], wo_ref[3 * gw:4 * gw]))
    mod = mod_ref[0]
    x1 = x_ref[...] + mod[2:3] * mix
    xo_ref[...] = x1
    hn = x1 * lax.rsqrt(jnp.mean(x1 * x1, axis=-1, keepdims=True) + EPS) * g_ref[...]
    h2 = hn * (1.0 + mod[4:5]) + mod[3:4]
    h2_ref[...] = h2.astype(BF16)

    a_hi, a_lo = _split_bf16(h2)
    w_hi = wr_hi_ref[...]
    lg = _dot_nt(w_hi, a_hi) + _dot_nt(w_hi, a_lo) + _dot_nt(wr_lo_ref[...], a_hi) + rb_ref[...]
    gl = lg[0:N_GROUPS]
    rows = gl.shape[1]
    iota8 = lax.broadcasted_iota(jnp.int32, (N_GROUPS, rows), 0)
    gmax = jnp.max(gl, axis=0, keepdims=True)
    grp = jnp.min(jnp.where(gl == gmax, iota8, N_GROUPS), axis=0, keepdims=True)
    p_grp = 1.0 / jnp.sum(jnp.exp(gl - gmax), axis=0, keepdims=True)
    esel = jnp.zeros((EXPERTS_PER_GROUP, rows), F32)
    for gi in range(N_GROUPS):
        lo = N_GROUPS + gi * EXPERTS_PER_GROUP
        esel = jnp.where(grp == gi, lg[lo:lo + EXPERTS_PER_GROUP], esel)
    ee = jnp.exp(esel - jnp.max(esel, axis=0, keepdims=True))
    ps = ee / jnp.sum(ee, axis=0, keepdims=True)
    p1 = jnp.max(ps, axis=0, keepdims=True)
    i1 = jnp.min(jnp.where(ps == p1, iota8, EXPERTS_PER_GROUP), axis=0, keepdims=True)
    ps2 = jnp.where(iota8 == i1, -1.0, ps)
    p2 = jnp.max(ps2, axis=0, keepdims=True)
    i2 = jnp.min(jnp.where(ps2 == p2, iota8, EXPERTS_PER_GROUP), axis=0, keepdims=True)
    den = p1 + p2
    g1 = p_grp * p1 / den
    g2 = p_grp * p2 / den
    e1 = (grp * EXPERTS_PER_GROUP + i1).astype(F32)
    e2 = (grp * EXPERTS_PER_GROUP + i2).astype(F32)
    zero = jnp.zeros_like(g1)
    route_ref[...] = jnp.concatenate([g1, g2, e1, e2, zero, zero, zero, zero], axis=0)


def _out_router_call(oa, ob, oc, od, wo, x, mod, g, wr_hi, wr_lo, rb, bsz, seq):
    n = bsz * seq
    rt = min(ROW_TILE, seq)
    nt = seq // rt
    row = lambda width: pl.BlockSpec((rt, width), lambda t: (t, 0))
    full = lambda a: pl.BlockSpec(a.shape, lambda t: (0,) * a.ndim)
    return pl.pallas_call(
        _out_router_kernel,
        grid=(n // rt,),
        in_specs=[
            row(GROUP_WIDTH), row(GROUP_WIDTH), row(GROUP_WIDTH), row(GROUP_WIDTH), full(wo),
            row(D_MODEL),
            pl.BlockSpec((1, ADA_CHUNKS, D_MODEL), lambda t: (t // nt, 0, 0)),
            full(g), full(wr_hi), full(wr_lo), full(rb),
        ],
        out_specs=[row(D_MODEL), row(D_MODEL), pl.BlockSpec((8, rt), lambda t: (0, t))],
        out_shape=[
            jax.ShapeDtypeStruct((n, D_MODEL), F32),
            jax.ShapeDtypeStruct((n, D_MODEL), BF16),
            jax.ShapeDtypeStruct((8, n), F32),
        ],
        compiler_params=_cparams(("arbitrary",)),
        name="out_proj_router",
    )(oa, ob, oc, od, wo, x, mod, g, wr_hi, wr_lo, rb)


def _dispatch_kernel(np_ref, slot_ref, lp_ref, h_ref, init_hbm, xs_hbm, sorted_buf, sem):
    del init_hbm
    t = pl.program_id(0)
    rows_l = sorted_buf.shape[0]
    lp = lp_ref[0]
    pos = lax.broadcasted_iota(jnp.int32, (rows_l, lp.shape[1]), 0)
    hit = pos == lp[0:1, :]
    for k in range(1, TOP_K):
        hit = hit | (pos == lp[k:k + 1, :])
    perm = jnp.where(hit, 1.0, 0.0).astype(BF16)
    ordered = _dot(perm, h_ref[...])
    sorted_buf[...] = _pack_bf16_pair(ordered[:, 0:D_MODEL // 2], ordered[:, D_MODEL // 2:D_MODEL])

    def piece(q):
        src = sorted_buf.at[pl.ds(pl.multiple_of(q * PIECE, PIECE), PIECE)]
        slot = pl.multiple_of(slot_ref[0, 0, q], PIECE)
        return pltpu.make_async_copy(src, xs_hbm.at[pl.ds(slot, PIECE)], sem)

    def issue(q, carry):
        piece(q).start()
        return carry

    def drain(q, carry):
        piece(q).wait()
        return carry
    lax.fori_loop(0, np_ref[t], issue, 0)
    lax.fori_loop(0, np_ref[t], drain, 0)


def _dispatch_call(n_pieces, piece_slot, lp_rows, h2, n_slots):
    n = h2.shape[0]
    n_tiles, _, q_max = piece_slot.shape
    td = n // n_tiles
    init = jnp.zeros((n_slots, D_MODEL // 2), jnp.int32)
    grid_spec = pltpu.PrefetchScalarGridSpec(
        num_scalar_prefetch=1,
        grid=(n_tiles,),
        in_specs=[
            pl.BlockSpec((1, 1, q_max), lambda t, npc: (t, 0, 0), memory_space=pltpu.SMEM),
            pl.BlockSpec((1, 8, td), lambda t, npc: (t, 0, 0)),
            pl.BlockSpec((td, D_MODEL), lambda t, npc: (t, 0)),
            pl.BlockSpec(memory_space=pl.ANY),
        ],
        out_specs=pl.BlockSpec(memory_space=pl.ANY),
        scratch_shapes=[pltpu.VMEM((q_max * PIECE, D_MODEL // 2), jnp.int32), pltpu.SemaphoreType.DMA],
    )
    return pl.pallas_call(
        _dispatch_kernel,
        grid_spec=grid_spec,
        out_shape=jax.ShapeDtypeStruct((n_slots, D_MODEL // 2), jnp.int32),
        input_output_aliases={4: 0},
        compiler_params=_cparams(("arbitrary",)),
        name="moe_dispatch",
    )(n_pieces, piece_slot, lp_rows, h2, init)


def _expert_kernel(be_ref, nact_ref, x_ref, wg_ref, wu_ref, wd_ref, y_ref, wgb, wub, wdb):
    i = pl.program_id(0)

    @pl.when(i < nact_ref[0])
    def _():
        prev = be_ref[jnp.maximum(i - 1, 0)]

        @pl.when((i == 0) | (be_ref[i] != prev))
        def _():
            wgb[...] = wg_ref[0, 0].astype(BF16)
            wub[...] = wu_ref[0, 0].astype(BF16)
            wdb[...] = wd_ref[0, 0].astype(BF16)

        half = D_MODEL // 2
        x_lo, x_hi = _unpack_bf16_pair(x_ref[...])
        gate = _dot(x_lo, wgb[0:half]) + _dot(x_hi, wgb[half:D_MODEL])
        up = _dot(x_lo, wub[0:half]) + _dot(x_hi, wub[half:D_MODEL])
        hid = gate * _sigmoid(gate) * up
        y = _dot(hid.astype(BF16), wdb[...])
        y_ref[...] = _pack_bf16_pair(y[:, 0:half], y[:, half:D_MODEL])

    @pl.when(i >= nact_ref[0])
    def _():
        y_ref[...] = jnp.zeros_like(y_ref)


def _expert_call(block_expert, n_active, x_slots, w_gate, w_up, w_down, layer):
    n_blocks = block_expert.shape[0]
    tm = MOE_TILE
    w_idx = lambda i, be, na: (layer, be[i], 0, 0)
    grid_spec = pltpu.PrefetchScalarGridSpec(
        num_scalar_prefetch=2,
        grid=(n_blocks,),
        in_specs=[
            pl.BlockSpec((tm, D_MODEL // 2), lambda i, be, na: (jnp.minimum(i, na[0] - 1), 0)),
            pl.BlockSpec((1, 1, D_MODEL, D_EXPERT), w_idx),
            pl.BlockSpec((1, 1, D_MODEL, D_EXPERT), w_idx),
            pl.BlockSpec((1, 1, D_EXPERT, D_MODEL), w_idx),
        ],
        out_specs=pl.BlockSpec((tm, D_MODEL // 2), lambda i, be, na: (i, 0)),
        scratch_shapes=[
            pltpu.VMEM((D_MODEL, D_EXPERT), BF16),
            pltpu.VMEM((D_MODEL, D_EXPERT), BF16),
            pltpu.VMEM((D_EXPERT, D_MODEL), BF16),
        ],
    )
    return pl.pallas_call(
        _expert_kernel,
        grid_spec=grid_spec,
        out_shape=jax.ShapeDtypeStruct((n_blocks * tm, D_MODEL // 2), jnp.int32),
        compiler_params=_cparams(("arbitrary",)),
        name="expert_mlp",
    )(block_expert, n_active, x_slots, w_gate, w_up, w_down)


def _combine_kernel(slot_ref, y_hbm, x_ref, route_ref, mod_ref, fg_ref, o_ref, ybuf, sem, *, final):
    n_q = slot_ref.shape[2]

    def issue(q, carry):
        dst = ybuf.at[pl.ds(pl.multiple_of(q * PIECE, PIECE), PIECE)]
        slot = pl.multiple_of(slot_ref[0, 0, q], PIECE)
        pltpu.make_async_copy(y_hbm.at[pl.ds(slot, PIECE)], dst, sem).start()
        return carry
    lax.fori_loop(0, n_q, issue, 0)
    pltpu.make_async_copy(y_hbm.at[pl.ds(0, ybuf.shape[0])], ybuf, sem).wait()

    rt = route_ref[...]
    pos = lax.broadcasted_iota(jnp.int32, (rt.shape[0], ybuf.shape[0]), 1)
    weights = jnp.zeros(pos.shape, F32)
    for k in range(TOP_K):
        lp_k = rt[:, 4 + k:5 + k].astype(jnp.int32)
        weights = jnp.where(pos == lp_k, rt[:, k:k + 1], weights)
    y_lo, y_hi = _unpack_bf16_pair(ybuf[...])
    wb = weights.astype(BF16)
    y = jnp.concatenate([_dot(wb, y_lo), _dot(wb, y_hi)], axis=-1)
    x2 = x_ref[...] + mod_ref[0][5:6] * y
    if final:
        x2 = x2 * lax.rsqrt(jnp.mean(x2 * x2, axis=-1, keepdims=True) + EPS) * fg_ref[...]
    o_ref[...] = x2


def _combine_call(piece_slot, y_slots, x, route_lp, mod, fg, bsz, seq, final):
    n = bsz * seq
    n_tiles, _, q_max = piece_slot.shape
    tc = n // n_tiles
    nt = seq // tc
    return pl.pallas_call(
        functools.partial(_combine_kernel, final=final),
        grid=(n_tiles,),
        in_specs=[
            pl.BlockSpec((1, 1, q_max), lambda t: (t, 0, 0), memory_space=pltpu.SMEM),
            pl.BlockSpec(memory_space=pl.ANY),
            pl.BlockSpec((tc, D_MODEL), lambda t: (t, 0)),
            pl.BlockSpec((tc, 8), lambda t: (t, 0)),
            pl.BlockSpec((1, ADA_CHUNKS, D_MODEL), lambda t: (t // nt, 0, 0)),
            pl.BlockSpec((1, D_MODEL), lambda t: (0, 0)),
        ],
        out_specs=pl.BlockSpec((tc, D_MODEL), lambda t: (t, 0)),
        out_shape=jax.ShapeDtypeStruct((n, D_MODEL), F32),
        scratch_shapes=[pltpu.VMEM((q_max * PIECE, D_MODEL // 2), jnp.int32), pltpu.SemaphoreType.DMA],
        compiler_params=_cparams(("arbitrary",)),
        name="moe_combine",
    )(piece_slot, y_slots, x, route_lp, mod, fg)


def _t5_bucket(rel):
    nb = T5_BUCKETS // 2
    max_exact = nb // 2
    bucket = jnp.where(rel > 0, nb, 0)
    n = jnp.abs(rel)
    large = max_exact + (jnp.log(jnp.maximum(n, 1).astype(F32) / max_exact)
                         / math.log(T5_MAX_DIST / max_exact) * (nb - max_exact)).astype(jnp.int32)
    large = jnp.minimum(large, nb - 1)
    return bucket + jnp.where(n < max_exact, n, large)


def _t5_tiles(t5_table, tq, tk):
    assert tk >= T5_MAX_DIST and tk % tq == 0
    par = jnp.arange(tk // tq, dtype=jnp.int32)[:, None, None, None]
    back = jnp.arange(2, dtype=jnp.int32)[None, :, None, None]
    r = jnp.arange(tq, dtype=jnp.int32)[None, None, :, None]
    c = jnp.arange(tk, dtype=jnp.int32)[None, None, None, :]
    rel = c - back * tk - (r + par * tq)
    far = t5_table[_t5_bucket(jnp.array(-4 * T5_MAX_DIST, jnp.int32))]
    bucket = _t5_bucket(rel)[..., None]
    b = jnp.zeros(rel.shape + (t5_table.shape[1],), F32)
    for k in range(T5_BUCKETS):
        b = jnp.where(bucket == k, t5_table[k], b)
    b = b - far
    return (b.transpose(4, 0, 1, 2, 3) * LOG2E).astype(F32)


def _place(cols, width, offset):
    z = jnp.zeros((cols.shape[0], width), cols.dtype)
    return z.at[:, offset:offset + cols.shape[1]].set(cols)


def _rot_half(cols):
    half = cols.shape[1] // 2
    return jnp.concatenate([-cols[:, half:], cols[:, :half]], axis=1)


def _prep_in_proj(w_in):
    cq = _place(w_in[:, 0:MLA_Q_LORA], _CQ_PAD, 0)
    ckv = w_in[:, MLA_Q_LORA:MLA_Q_LORA + MLA_KV_LORA]
    kr = w_in[:, MLA_Q_LORA + MLA_KV_LORA:_O_FOX]
    kr_p = _place(kr, LANES, MLA_NOPE)
    kr_rot = _place(_rot_half(kr), LANES, MLA_NOPE)
    fox = w_in[:, _O_FOX:_O_FOX + 3 * GROUP_WIDTH]
    wf = w_in[:, _O_FOX + 3 * GROUP_WIDTH:_O_CONV]
    conv = w_in[:, _O_CONV:_O_DIFF]
    diff = w_in[:, _O_DIFF:_O_DIFF + 3 * GROUP_WIDTH]
    w = jnp.concatenate([cq, ckv, kr_p, kr_rot, fox, conv, diff], axis=1).astype(BF16)
    wf_t = _place(wf, 16, 0).T.astype(BF16)
    return w, wf_t


def _prep_mla(w_uq, w_ukv):
    hd = MLA_NOPE + MLA_ROPE
    plain, rot, k_nope, v = [], [], [], []
    for h in range(4):
        wh = w_uq[:, h * hd:(h + 1) * hd]
        plain.append(_place(wh, LANES, 0))
        rot.append(_place(_rot_half(wh[:, MLA_NOPE:]), LANES, MLA_NOPE))
        kv = w_ukv[:, h * 2 * HEAD_DIM:(h + 1) * 2 * HEAD_DIM]
        k_nope.append(_place(kv[:, :MLA_NOPE], LANES, 0))
        v.append(kv[:, MLA_NOPE:])
    wuq = jnp.concatenate(plain + rot, axis=1)
    wuq = jnp.concatenate([wuq, jnp.zeros((_CQ_PAD - MLA_Q_LORA, wuq.shape[1]), wuq.dtype)], axis=0)
    wukv = jnp.concatenate(k_nope + v, axis=1)
    return wuq.astype(BF16), wukv.astype(BF16)


def _rope_tables(positions):
    half = MLA_ROPE // 2
    inv_freq = ROPE_THETA ** (-jnp.arange(half, dtype=F32) / half)
    ang = positions.astype(F32).reshape(-1)[:, None] * inv_freq
    cos, sin = jnp.cos(ang), jnp.sin(ang)
    n = ang.shape[0]
    cos_t = jnp.ones((n, LANES), F32).at[:, MLA_NOPE:MLA_NOPE + MLA_ROPE].set(jnp.concatenate([cos, cos], 1))
    sin_t = jnp.zeros((n, LANES), F32).at[:, MLA_NOPE:MLA_NOPE + MLA_ROPE].set(jnp.concatenate([sin, sin], 1))
    return cos_t, sin_t


def _piece_rows(tile):
    rows = tile * TOP_K + N_EXPERTS * (PIECE - 1)
    return -(-rows // LANES) * LANES


def _dispatch(expert, n_tok, tile):
    tm = MOE_TILE
    n_assign = n_tok * TOP_K
    per_tile = tile * TOP_K
    n_tiles = n_tok // tile
    q_max = _piece_rows(tile) // PIECE
    chunk = LANES
    experts = jnp.arange(N_EXPERTS, dtype=jnp.int32)
    e_flat = expert.reshape(n_assign)
    onehot = e_flat[:, None] == experts[None, :]
    oh = onehot.astype(BF16).reshape(n_assign // chunk, chunk, N_EXPERTS)
    tri = (jnp.arange(chunk)[:, None] >= jnp.arange(chunk)[None, :]).astype(BF16)
    within = jnp.einsum("ij,cjk->cik", tri, oh, preferred_element_type=F32)
    total = within[:, chunk - 1, :]
    before = jnp.cumsum(total, axis=0) - total
    running = (within + before[:, None, :]).reshape(n_tiles, per_tile, N_EXPERTS)
    onehot = onehot.reshape(n_tiles, per_tile, N_EXPERTS)

    run_end = running[:, per_tile - 1, :]
    run_start = jnp.concatenate([jnp.zeros((1, N_EXPERTS), F32), run_end[:-1]], axis=0)
    pieces = ((run_end - run_start).astype(jnp.int32) + PIECE - 1) // PIECE
    run_rows = pieces * PIECE
    rows_before = jnp.cumsum(run_rows, axis=0) - run_rows
    counts = rows_before[-1] + run_rows[-1]
    padded = (counts + tm - 1) // tm * tm
    pad_ends = jnp.cumsum(padded)
    pad_starts = pad_ends - padded

    piece_end = jnp.cumsum(pieces, axis=1)
    piece_first = piece_end - pieces
    local_start = (piece_first * PIECE).astype(F32)
    local = running - 1.0 - run_start[:, None, :] + local_start[:, None, :]
    lp = jnp.sum(jnp.where(onehot, local, 0.0), axis=2).astype(jnp.int32)
    lp = lp.reshape(n_tok, TOP_K)

    q = jnp.arange(q_max, dtype=jnp.int32)[None, :]
    e_of_q = jnp.sum((piece_end[:, None, :] <= q[:, :, None]).astype(jnp.int32), axis=2)
    pick = e_of_q[:, :, None] == experts[None, None, :]
    run_slot = pad_starts[None, :] + rows_before
    first_q = jnp.sum(jnp.where(pick, piece_first[:, None, :], 0), axis=2)
    slot_q = jnp.sum(jnp.where(pick, run_slot[:, None, :], 0), axis=2)
    piece_slot = jnp.where(e_of_q < N_EXPERTS, slot_q + (q - first_q) * PIECE, 0)
    n_pieces = piece_end[:, N_EXPERTS - 1]

    n_blocks = -(-(n_assign + N_EXPERTS * (n_tiles * (PIECE - 1) + tm - 1)) // tm)
    block_start = jnp.arange(n_blocks, dtype=jnp.int32) * tm
    block_expert = jnp.minimum(
        jnp.sum((pad_ends[None, :] <= block_start[:, None]).astype(jnp.int32), axis=1), N_EXPERTS - 1)
    n_active = (pad_ends[-1] // tm).astype(jnp.int32).reshape(1)
    return (block_expert, n_active, lp, piece_slot.reshape(n_tiles, 1, q_max).astype(jnp.int32),
            n_pieces.astype(jnp.int32), n_blocks * tm)


def kernel(x, c, positions, t5_table, ada_w, ada_b, norm_mix_g, norm_ffn_g, w_in, mla_q_norm_g, mla_w_uq, mla_kv_norm_g, mla_w_ukv, fox_forget_b, conv_w, diff_lambda, diff_subln_g, w_out, router_group_w, router_group_b, router_expert_w, router_expert_b, expert_w_gate, expert_w_up, expert_w_down, final_norm_g):
    bsz, seq, d = x.shape
    n = bsz * seq
    tq = min(ATT_Q, seq)
    tk = min(ATT_K, seq)
    tc = min(CMB_TILE, seq)

    mods = _ada_call(c, ada_w, ada_b).reshape(DEPTH, bsz, ADA_CHUNKS, D_MODEL)
    cos_t, sin_t = _rope_tables(positions)
    bias_tiles = _t5_tiles(t5_table, tq, tk)
    xf = x.reshape(n, d)

    for layer in range(DEPTH):
        mod = mods[layer]
        w, wf_t = _prep_in_proj(w_in[layer])
        wuq, wukv = _prep_mla(mla_w_uq[layer], mla_w_ukv[layer])
        fb = _place(fox_forget_b[layer][None, :], 16, 0).T.astype(F32)
        qg = _place(mla_q_norm_g[layer][None, :], _CQ_PAD, 0)
        kvg = mla_kv_norm_g[layer][None, :]
        cw = _place(conv_w[layer].T, 8, 0).T
        (qa, ka, va, qb, kb, vb, nf, oc, qd, kd, vd) = _proj_call(
            xf, mod, norm_mix_g[layer][None, :], w, wf_t, fb, wuq, wukv, qg, kvg, cos_t, sin_t, cw, bsz, seq)

        nf_t = nf.reshape(bsz, 8, seq // tk, tk).transpose(0, 2, 1, 3)
        lam_init = 0.8 - 0.6 * math.exp(-0.3 * layer)
        lam_p = diff_lambda[layer].astype(F32)
        lam = jnp.exp(jnp.sum(lam_p[0] * lam_p[1])) - jnp.exp(jnp.sum(lam_p[2] * lam_p[3])) + lam_init
        lam_row = jnp.full((1, LANES), lam, F32)
        sg = (jnp.concatenate([diff_subln_g[layer]] * 2) * (1.0 - lam_init))[None, :].astype(F32)
        oa, ob, od = _attn_call(
            [("A", (qa, ka, va)), ("B", (qb, kb, vb, nf_t)), ("D", (qd, kd, vd, bias_tiles, lam_row, sg))],
            bsz, seq)

        wr = jnp.concatenate([router_group_w[layer], router_expert_w[layer]], axis=1)
        wr_t = _place(wr, 80, 0).T
        wr_hi = wr_t.astype(BF16)
        wr_lo = (wr_t - wr_hi.astype(F32)).astype(BF16)
        rb = _place(jnp.concatenate([router_group_b[layer], router_expert_b[layer]])[None, :], 80, 0).T
        x1, h2, route = _out_router_call(
            oa, ob, oc, od, w_out[layer].astype(BF16), xf, mod, norm_ffn_g[layer][None, :],
            wr_hi, wr_lo, rb.astype(F32), bsz, seq)

        route_t = route.T
        expert = route_t[:, 2:4].astype(jnp.int32)
        block_expert, n_active, lp, piece_slot, n_pieces, n_slots = _dispatch(expert, n, tc)
        lp_rows = jnp.full((n // tc, 8, tc), -1, jnp.int32).at[:, 0:TOP_K, :].set(
            lp.reshape(n // tc, tc, TOP_K).transpose(0, 2, 1))
        x_slots = _dispatch_call(n_pieces, piece_slot, lp_rows, h2, n_slots)
        y_slots = _expert_call(block_expert, n_active, x_slots,
                               expert_w_gate, expert_w_up, expert_w_down, layer)
        route_lp = jnp.concatenate([route_t[:, 0:4], lp.astype(F32), route_t[:, 6:8]], axis=1)
        xf = _combine_call(piece_slot, y_slots, x1, route_lp, mod, final_norm_g[None, :], bsz, seq,
                           final=(layer == DEPTH - 1))
    return xf.reshape(bsz, seq, d)
```

```python
import functools
import math

import jax
import jax.numpy as jnp
from jax import lax
from jax.experimental import pallas as pl
from jax.experimental.pallas import tpu as pltpu

F32 = jnp.float32
BF16 = jnp.bfloat16

D_MODEL = 1024
DEPTH = 4
CHUNK = 64
EPS = 1e-6
HEAD_DIM = 64
GROUP_WIDTH = 256
MLA_NOPE = 64
MLA_ROPE = 32
MLA_Q_LORA = 192
MLA_KV_LORA = 128
ROPE_THETA = 10000.0
DIFF_HALF = 32
T5_BUCKETS = 32
T5_MAX_DIST = 128
N_GROUPS = 8
EXPERTS_PER_GROUP = 8
N_EXPERTS = 64
TOP_K = 2
D_EXPERT = 512
ADA_CHUNKS = 6

LANES = 128
LOG2E = 1.4426950408889634
NEG = -1e30

_O_MLA = 0
_O_FOX = MLA_Q_LORA + MLA_KV_LORA + MLA_ROPE
_O_CONV = _O_FOX + 3 * GROUP_WIDTH + 4
_O_DIFF = _O_CONV + 3 * GROUP_WIDTH

_CQ_PAD = 256
_W_A = _CQ_PAD + MLA_KV_LORA + 2 * LANES
_W_B = 3 * GROUP_WIDTH
_W_C = 3 * GROUP_WIDTH
_W_D = 3 * GROUP_WIDTH
_P_TOTAL = _W_A + _W_B + _W_C + _W_D

ROW_TILE = 512
ATT_Q = 256
ATT_K = 256
MOE_TILE = 256
CMB_TILE = 512
PIECE = 8
VMEM_LIMIT = 56 * 1024 * 1024


def _cparams(sem):
    return pltpu.CompilerParams(dimension_semantics=sem, vmem_limit_bytes=VMEM_LIMIT)


def _split_bf16(a):
    hi = a.astype(BF16)
    lo = (a - hi.astype(F32)).astype(BF16)
    return hi, lo


def _dot(a, b):
    return jnp.dot(a, b, preferred_element_type=F32)


def _dot_nt(a, b):
    return lax.dot_general(a, b, (((1,), (1,)), ((), ())), preferred_element_type=F32)


def _sigmoid(z):
    return 1.0 / (1.0 + jnp.exp(-z))


_HIGH16 = -65536


def _pack_bf16_pair(lo, hi):
    lo_bits = pltpu.bitcast(lo.astype(BF16).astype(F32), jnp.int32)
    hi_bits = pltpu.bitcast(hi.astype(BF16).astype(F32), jnp.int32)
    sixteen = jnp.full(lo_bits.shape, 16, jnp.int32)
    return (hi_bits & _HIGH16) | lax.shift_right_logical(lo_bits, sixteen)


def _unpack_bf16_pair(packed):
    lo = pltpu.bitcast(lax.shift_left(packed, jnp.full(packed.shape, 16, jnp.int32)), F32)
    hi = pltpu.bitcast(packed & _HIGH16, F32)
    return lo.astype(BF16), hi.astype(BF16)


def _ada_kernel(c_ref, w_ref, b_ref, o_ref):
    c = c_ref[...]
    cond = c * _sigmoid(c)
    c_hi, c_lo = _split_bf16(cond)
    w_hi, w_lo = _split_bf16(w_ref[0])
    o_ref[0] = _dot(c_hi, w_hi) + _dot(c_hi, w_lo) + _dot(c_lo, w_hi) + b_ref[0]


def _ada_call(c, ada_w, ada_b):
    bsz = c.shape[0]
    n_col = ADA_CHUNKS * D_MODEL // D_MODEL
    return pl.pallas_call(
        _ada_kernel,
        grid=(DEPTH, n_col),
        in_specs=[
            pl.BlockSpec((bsz, D_MODEL), lambda l, j: (0, 0)),
            pl.BlockSpec((1, D_MODEL, D_MODEL), lambda l, j: (l, 0, j)),
            pl.BlockSpec((1, 1, D_MODEL), lambda l, j: (l, 0, j)),
        ],
        out_specs=pl.BlockSpec((1, bsz, D_MODEL), lambda l, j: (l, 0, j)),
        out_shape=jax.ShapeDtypeStruct((DEPTH, bsz, ADA_CHUNKS * D_MODEL), F32),
        compiler_params=_cparams(("arbitrary", "arbitrary")),
        name="ada_mod",
    )(c, ada_w, ada_b.reshape(DEPTH, 1, ADA_CHUNKS * D_MODEL))


def _values_with_ones(v):
    lane = lax.broadcasted_iota(jnp.int32, (1, LANES), 1)
    low_half = lane < HEAD_DIM
    slabs = []
    for head in range(4):
        pair = v[:, (head // 2) * LANES:(head // 2 + 1) * LANES]
        keep = low_half if head % 2 == 0 else jnp.logical_not(low_half)
        slabs.append(jnp.where(keep, pair, 1.0))
    return jnp.concatenate(slabs, axis=-1).astype(BF16)


def _proj_kernel(x_ref, mod_ref, g_ref, w_ref, wf_ref, fb_ref, wuq_ref, wukv_ref, qg_ref, kvg_ref,
                 cos_ref, sin_ref, cw_ref,
                 qa_ref, ka_ref, va_ref, qb_ref, kb_ref, vb_ref, nf_ref, oc_ref, qd_ref, kd_ref, vd_ref,
                 zc_ref, fc_ref, *, scale_a, scale_b, scale_d):
    t = pl.program_id(1)
    rows = x_ref.shape[0]

    @pl.when(t == 0)
    def _():
        zc_ref[...] = jnp.zeros_like(zc_ref)
        fc_ref[...] = jnp.zeros_like(fc_ref)

    x = x_ref[...]
    mod = mod_ref[0]
    hn = x * lax.rsqrt(jnp.mean(x * x, axis=-1, keepdims=True) + EPS) * g_ref[...]
    h = hn * (1.0 + mod[1:2]) + mod[0:1]
    hb = h.astype(BF16)

    z = _dot_nt(wf_ref[...], hb) + fb_ref[...]
    log_f = jnp.minimum(z, 0.0) - jnp.log(1.0 + jnp.exp(-jnp.abs(z)))
    r_i = lax.broadcasted_iota(jnp.int32, (rows, rows), 0)
    c_i = lax.broadcasted_iota(jnp.int32, (rows, rows), 1)
    tri = jnp.where(r_i <= c_i, 1.0, 0.0).astype(BF16)
    f_hi = log_f.astype(BF16)
    rem = log_f - f_hi.astype(F32)
    f_mid = rem.astype(BF16)
    f_lo = (rem - f_mid.astype(F32)).astype(BF16)
    cum = _dot(f_hi, tri) + _dot(f_mid, tri) + _dot(f_lo, tri) + fc_ref[:, 0:1]
    fc_ref[...] = jnp.broadcast_to(cum[:, rows - 1:rows], fc_ref.shape)
    nf_ref[0] = cum[0:8] * (-LOG2E)

    pa = _dot(hb, w_ref[:, 0:_W_A])
    cq = pa[:, 0:_CQ_PAD]
    ckv = pa[:, _CQ_PAD:_CQ_PAD + MLA_KV_LORA]
    kr = pa[:, _CQ_PAD + MLA_KV_LORA:_CQ_PAD + MLA_KV_LORA + LANES]
    krr = pa[:, _CQ_PAD + MLA_KV_LORA + LANES:_W_A]
    cqn = cq * lax.rsqrt(jnp.sum(cq * cq, axis=-1, keepdims=True) * (1.0 / MLA_Q_LORA) + EPS) * qg_ref[...]
    ckvn = ckv * lax.rsqrt(jnp.mean(ckv * ckv, axis=-1, keepdims=True) + EPS) * kvg_ref[...]
    q2 = _dot(cqn.astype(BF16), wuq_ref[...])
    kv2 = _dot(ckvn.astype(BF16), wukv_ref[...])
    cos = cos_ref[...]
    sin = sin_ref[...]
    cos4 = jnp.concatenate([cos] * 4, axis=-1)
    sin4 = jnp.concatenate([sin] * 4, axis=-1)
    qa = (q2[:, 0:512] * cos4 + q2[:, 512:1024] * sin4) * (scale_a * LOG2E)
    k_rope = kr * cos + krr * sin
    ka = kv2[:, 0:512] + jnp.concatenate([k_rope] * 4, axis=-1)
    qa_ref[...] = qa.astype(BF16)
    ka_ref[...] = ka.astype(BF16)
    va_ref[...] = _values_with_ones(kv2[:, 512:768])

    pb = _dot(hb, w_ref[:, _W_A:_W_A + _W_B])
    qb_ref[...] = (pb[:, 0:256] * (scale_b * LOG2E)).astype(BF16)
    kb_ref[...] = pb[:, 256:512].astype(BF16)
    vb_ref[...] = _values_with_ones(pb[:, 512:768])

    pc = _dot(hb, w_ref[:, _W_A + _W_B:_W_A + _W_B + _W_C])
    zz = pc[:, 256:512] * pc[:, 512:768]
    ext = jnp.concatenate([zc_ref[...], zz], axis=0)
    cw = cw_ref[...]
    conv = zz * cw[2:3] + ext[7:rows + 7] * cw[1:2] + ext[6:rows + 6] * cw[0:1]
    zc_ref[...] = zz[rows - 8:rows]
    oc_ref[...] = (pc[:, 0:256] * conv).astype(BF16)

    pd = _dot(hb, w_ref[:, _W_A + _W_B + _W_C:_P_TOTAL])
    qd_ref[...] = (pd[:, 0:256] * (scale_d * LOG2E)).astype(BF16)
    kd_ref[...] = pd[:, 256:512].astype(BF16)
    vd_ref[...] = _values_with_ones(pd[:, 512:768])


def _proj_call(x, mod, g, w, wf, fb, wuq, wukv, qg, kvg, cos_t, sin_t, cw, bsz, seq):
    n = bsz * seq
    rt = min(ROW_TILE, seq)
    nt = seq // rt
    row = lambda width: pl.BlockSpec((rt, width), lambda b, t: (b * nt + t, 0))
    full = lambda a: pl.BlockSpec(a.shape, lambda b, t: (0,) * a.ndim)
    bf = lambda width: jax.ShapeDtypeStruct((n, width), BF16)
    kern = functools.partial(
        _proj_kernel,
        scale_a=(MLA_NOPE + MLA_ROPE) ** -0.5, scale_b=HEAD_DIM ** -0.5, scale_d=DIFF_HALF ** -0.5)
    return pl.pallas_call(
        kern,
        grid=(bsz, nt),
        in_specs=[
            row(D_MODEL),
            pl.BlockSpec((1, ADA_CHUNKS, D_MODEL), lambda b, t: (b, 0, 0)),
            full(g), full(w), full(wf), full(fb), full(wuq), full(wukv), full(qg), full(kvg),
            row(LANES), row(LANES), full(cw),
        ],
        out_specs=[
            row(512), row(512), row(512), row(256), row(256), row(512),
            pl.BlockSpec((1, 8, rt), lambda b, t: (b, 0, t)),
            row(256), row(256), row(256), row(512),
        ],
        out_shape=[
            bf(512), bf(512), bf(512), bf(256), bf(256), bf(512),
            jax.ShapeDtypeStruct((bsz, 8, seq), F32),
            bf(256), bf(256), bf(256), bf(512),
        ],
        scratch_shapes=[pltpu.VMEM((8, GROUP_WIDTH), F32), pltpu.VMEM((16, LANES), F32)],
        compiler_params=_cparams(("arbitrary", "arbitrary")),
        name="norm_in_proj",
    )(x, mod, g, w, wf, fb, wuq, wukv, qg, kvg, cos_t, sin_t, cw)


def _tile_update(q, k_t, v_t, m_ref, acc_ref, bias=None, mask=None):
    s = _dot_nt(q, k_t)
    if bias is not None:
        s = s + bias
    if mask is not None:
        s = jnp.where(mask, s, NEG)
    m_prev = m_ref[...]
    m_new = jnp.maximum(m_prev, jnp.max(s, axis=-1, keepdims=True))
    alpha = jnp.exp2(m_prev - m_new)
    p = jnp.exp2(s - jnp.concatenate([m_new] * (s.shape[1] // LANES), axis=-1))
    acc_ref[...] = alpha * acc_ref[...] + _dot(p.astype(BF16), v_t)
    m_ref[...] = m_new


_ATTN_INPUTS = {"A": 3, "B": 4, "D": 6}
_ATTN_MAPS = {"A": 4, "B": 4, "D": 8}


def _attn_kernel(*refs, kinds, tq, tk):
    pos = 0
    ins, outs, q_scrs, m_refs, acc_refs = {}, {}, {}, {}, {}
    for kd in kinds:
        ins[kd] = refs[pos:pos + _ATTN_INPUTS[kd]]
        pos += _ATTN_INPUTS[kd]
    for kd in kinds:
        outs[kd] = refs[pos]
        pos += 1
    for kd in kinds:
        nm = _ATTN_MAPS[kd]
        q_scrs[kd] = refs[pos]
        m_refs[kd] = refs[pos + 1:pos + 1 + nm]
        acc_refs[kd] = refs[pos + 1 + nm:pos + 1 + 2 * nm]
        pos += 1 + 2 * nm
    i = pl.program_id(1)
    ratio = tk // tq
    jd = i // ratio
    par = i % ratio
    off = par * tq
    lane = lax.broadcasted_iota(jnp.int32, (1, LANES), 1)
    low_half = lane < HEAD_DIM
    r_i = lax.broadcasted_iota(jnp.int32, (tq, tk), 0) + off
    c_i = lax.broadcasted_iota(jnp.int32, (tq, tk), 1)
    shift = CHUNK.bit_length() - 1
    diag_masks = {
        "frame": c_i <= r_i,
        "chunk": jnp.right_shift(c_i, shift) <= jnp.right_shift(r_i, shift),
    }

    maps = []
    for kd in kinds:
        q_ref, q_scr = ins[kd][0], q_scrs[kd]
        for head in range(4):
            slab = head // 2
            if kd == "A":
                q_scr[head] = q_ref[:, head * LANES:(head + 1) * LANES]
                maps.append((kd, head, head, head))
            elif kd == "B":
                q = q_ref[:, slab * LANES:(slab + 1) * LANES]
                sel = low_half if head % 2 == 0 else jnp.logical_not(low_half)
                q_scr[head] = jnp.where(sel, q, jnp.zeros_like(q))
                maps.append((kd, head, slab, head))
            else:
                q = q_ref[:, slab * LANES:(slab + 1) * LANES]
                for mp in range(2):
                    lo = (2 * (head % 2) + mp) * DIFF_HALF
                    sel = (lane >= lo) & (lane < lo + DIFF_HALF)
                    q_scr[2 * head + mp] = jnp.where(sel, q, jnp.zeros_like(q))
                    maps.append((kd, 2 * head + mp, slab, head))
        for idx in range(_ATTN_MAPS[kd]):
            m_refs[kd][idx][...] = jnp.full(m_refs[kd][idx].shape, NEG, F32)
            acc_refs[kd][idx][...] = jnp.zeros(acc_refs[kd][idx].shape, F32)

    def step(j, mode):
        start = pl.multiple_of(j * tk, tk)
        for kd, idx, k_slab, head in maps:
            k_t = ins[kd][1][pl.ds(start, tk), k_slab * LANES:(k_slab + 1) * LANES]
            v_t = ins[kd][2][pl.ds(start, tk), head * LANES:(head + 1) * LANES]
            bias = None
            if kd == "B":
                bias = ins[kd][3][0, j][head:head + 1, :]
            elif kd == "D" and mode != "far":
                bias = ins[kd][3][head, par, 0 if mode == "diag" else 1]
            mask = diag_masks["frame" if kd == "B" else "chunk"] if mode == "diag" else None
            _tile_update(q_scrs[kd][idx], k_t, v_t, m_refs[kd][idx], acc_refs[kd][idx],
                         bias=bias, mask=mask)

    def far_pair(jj, carry):
        step(2 * jj, "far")
        step(2 * jj + 1, "far")
        return carry

    n_far = jnp.maximum(jd - 1, 0)
    lax.fori_loop(0, n_far // 2, far_pair, 0)

    @pl.when(n_far % 2 == 1)
    def _():
        step(n_far - 1, "far")

    @pl.when(jd >= 1)
    def _():
        step(jd - 1, "prev")
        step(jd, "diag")

    @pl.when(jd == 0)
    def _():
        step(jd, "diag")

    def normalized(kd, idx):
        acc = acc_refs[kd][idx][...]
        return acc / pltpu.roll(acc, HEAD_DIM, axis=1)

    for kd in kinds:
        for slab in range(2):
            heads = []
            for sub in range(2):
                head = 2 * slab + sub
                if kd == "D":
                    lam = ins[kd][4][...]
                    heads.append(normalized(kd, 2 * head) - lam * normalized(kd, 2 * head + 1))
                else:
                    heads.append(normalized(kd, head))
            o_slab = jnp.where(low_half, heads[0], heads[1])
            if kd == "D":
                sq = o_slab * o_slab
                s_lo = jnp.sum(jnp.where(low_half, sq, 0.0), axis=-1, keepdims=True)
                s_hi = jnp.sum(jnp.where(low_half, 0.0, sq), axis=-1, keepdims=True)
                ms = jnp.where(low_half, s_lo, s_hi) * (1.0 / HEAD_DIM)
                o_slab = o_slab * lax.rsqrt(ms + EPS) * ins[kd][5][...]
            outs[kd][:, slab * LANES:(slab + 1) * LANES] = o_slab.astype(BF16)


def _attn_call(groups, bsz, seq):
    n = bsz * seq
    tq = min(ATT_Q, seq)
    tk = min(ATT_K, seq)
    nq = seq // tq
    kinds = tuple(kd for kd, _ in groups)
    operands, in_specs, scratch = [], [], []
    for kd, (q, k, v, *extra) in groups:
        operands += [q, k, v, *extra]
        in_specs += [
            pl.BlockSpec((tq, q.shape[1]), lambda b, i: (b * nq + i, 0)),
            pl.BlockSpec((seq, k.shape[1]), lambda b, i: (b, 0)),
            pl.BlockSpec((seq, v.shape[1]), lambda b, i: (b, 0)),
        ]
        if kd == "B":
            in_specs.append(pl.BlockSpec((1, seq // tk, 8, tk), lambda b, i: (b, 0, 0, 0)))
        else:
            in_specs += [pl.BlockSpec(e.shape, lambda b, i, nd=e.ndim: (0,) * nd) for e in extra]
        scratch += [pltpu.VMEM((_ATTN_MAPS[kd], tq, LANES), BF16)]
        scratch += [pltpu.VMEM((tq, LANES), F32) for _ in range(2 * _ATTN_MAPS[kd])]
    out_spec = pl.BlockSpec((tq, GROUP_WIDTH), lambda b, i: (b * nq + i, 0))
    return pl.pallas_call(
        functools.partial(_attn_kernel, kinds=kinds, tq=tq, tk=tk),
        grid=(bsz, nq),
        in_specs=in_specs,
        out_specs=[out_spec] * len(kinds),
        out_shape=[jax.ShapeDtypeStruct((n, GROUP_WIDTH), BF16)] * len(kinds),
        scratch_shapes=scratch,
        compiler_params=_cparams(("arbitrary", "arbitrary")),
        name="attn_" + "".join(kinds),
    )(*operands)


def _out_router_kernel(oa_ref, ob_ref, oc_ref, od_ref, wo_ref, x_ref, mod_ref, g_ref, wr_hi_ref, wr_lo_ref,
                       rb_ref, xo_ref, h2_ref, route_ref):
    gw = GROUP_WIDTH
    mix = (_dot(oa_ref[...], wo_ref[0:gw]) + _dot(ob_ref[...], wo_ref[gw:2 * gw])
           + _dot(oc_ref[...], wo_ref[2 * gw:3 * gw]) + _dot(od_ref[...], wo_ref[3 * gw:4 * gw]))
    mod = mod_ref[0]
    x1 = x_ref[...] + mod[2:3] * mix
    xo_ref[...] = x1
    hn = x1 * lax.rsqrt(jnp.mean(x1 * x1, axis=-1, keepdims=True) + EPS) * g_ref[...]
    h2 = hn * (1.0 + mod[4:5]) + mod[3:4]
    h2_ref[...] = h2.astype(BF16)

    a_hi, a_lo = _split_bf16(h2)
    w_hi = wr_hi_ref[...]
    lg = _dot_nt(w_hi, a_hi) + _dot_nt(w_hi, a_lo) + _dot_nt(wr_lo_ref[...], a_hi) + rb_ref[...]
    gl = lg[0:N_GROUPS]
    rows = gl.shape[1]
    iota8 = lax.broadcasted_iota(jnp.int32, (N_GROUPS, rows), 0)
    gmax = jnp.max(gl, axis=0, keepdims=True)
    grp = jnp.min(jnp.where(gl == gmax, iota8, N_GROUPS), axis=0, keepdims=True)
    p_grp = 1.0 / jnp.sum(jnp.exp(gl - gmax), axis=0, keepdims=True)
    esel = jnp.zeros((EXPERTS_PER_GROUP, rows), F32)
    for gi in range(N_GROUPS):
        lo = N_GROUPS + gi * EXPERTS_PER_GROUP
        esel = jnp.where(grp == gi, lg[lo:lo + EXPERTS_PER_GROUP], esel)
    ee = jnp.exp(esel - jnp.max(esel, axis=0, keepdims=True))
    ps = ee / jnp.sum(ee, axis=0, keepdims=True)
    p1 = jnp.max(ps, axis=0, keepdims=True)
    i1 = jnp.min(jnp.where(ps == p1, iota8, EXPERTS_PER_GROUP), axis=0, keepdims=True)
    ps2 = jnp.where(iota8 == i1, -1.0, ps)
    p2 = jnp.max(ps2, axis=0, keepdims=True)
    i2 = jnp.min(jnp.where(ps2 == p2, iota8, EXPERTS_PER_GROUP), axis=0, keepdims=True)
    den = p1 + p2
    g1 = p_grp * p1 / den
    g2 = p_grp * p2 / den
    e1 = (grp * EXPERTS_PER_GROUP + i1).astype(F32)
    e2 = (grp * EXPERTS_PER_GROUP + i2).astype(F32)
    zero = jnp.zeros_like(g1)
    route_ref[...] = jnp.concatenate([g1, g2, e1, e2, zero, zero, zero, zero], axis=0)


def _out_router_call(oa, ob, oc, od, wo, x, mod, g, wr_hi, wr_lo, rb, bsz, seq):
    n = bsz * seq
    rt = min(ROW_TILE, seq)
    nt = seq // rt
    row = lambda width: pl.BlockSpec((rt, width), lambda t: (t, 0))
    full = lambda a: pl.BlockSpec(a.shape, lambda t: (0,) * a.ndim)
    return pl.pallas_call(
        _out_router_kernel,
        grid=(n // rt,),
        in_specs=[
            row(GROUP_WIDTH), row(GROUP_WIDTH), row(GROUP_WIDTH), row(GROUP_WIDTH), full(wo),
            row(D_MODEL),
            pl.BlockSpec((1, ADA_CHUNKS, D_MODEL), lambda t: (t // nt, 0, 0)),
            full(g), full(wr_hi), full(wr_lo), full(rb),
        ],
        out_specs=[row(D_MODEL), row(D_MODEL), pl.BlockSpec((8, rt), lambda t: (0, t))],
        out_shape=[
            jax.ShapeDtypeStruct((n, D_MODEL), F32),
            jax.ShapeDtypeStruct((n, D_MODEL), BF16),
            jax.ShapeDtypeStruct((8, n), F32),
        ],
        compiler_params=_cparams(("arbitrary",)),
        name="out_proj_router",
    )(oa, ob, oc, od, wo, x, mod, g, wr_hi, wr_lo, rb)


def _dispatch_kernel(np_ref, slot_ref, lp_ref, h_ref, init_hbm, xs_hbm, sorted_buf, sem):
    del init_hbm
    t = pl.program_id(0)
    cur = t % 2
    rows_l = sorted_buf.shape[1]
    lp = lp_ref[0]
    pos = lax.broadcasted_iota(jnp.int32, (rows_l, lp.shape[1]), 0)
    hit = pos == lp[0:1, :]
    for k in range(1, TOP_K):
        hit = hit | (pos == lp[k:k + 1, :])
    perm = jnp.where(hit, 1.0, 0.0).astype(BF16)
    ordered = _dot(perm, h_ref[...])
    sorted_buf[cur] = _pack_bf16_pair(ordered[:, 0:D_MODEL // 2], ordered[:, D_MODEL // 2:D_MODEL])

    def piece(q, buf):
        src = sorted_buf.at[buf, pl.ds(pl.multiple_of(q * PIECE, PIECE), PIECE)]
        slot = pl.multiple_of(slot_ref[0, 0, q], PIECE)
        return pltpu.make_async_copy(src, xs_hbm.at[pl.ds(slot, PIECE)], sem.at[buf])

    def drain(count, buf):
        def body(q, carry):
            piece(q, buf).wait()
            return carry
        lax.fori_loop(0, count, body, 0)

    @pl.when(t > 0)
    def _():
        drain(np_ref[jnp.maximum(t - 1, 0)], 1 - cur)

    def issue(q, carry):
        piece(q, cur).start()
        return carry
    lax.fori_loop(0, np_ref[t], issue, 0)

    @pl.when(t + 1 == pl.num_programs(0))
    def _():
        drain(np_ref[t], cur)


def _dispatch_call(n_pieces, piece_slot, lp_rows, h2, n_slots):
    n = h2.shape[0]
    n_tiles, _, q_max = piece_slot.shape
    td = n // n_tiles
    init = jnp.zeros((n_slots, D_MODEL // 2), jnp.int32)
    grid_spec = pltpu.PrefetchScalarGridSpec(
        num_scalar_prefetch=1,
        grid=(n_tiles,),
        in_specs=[
            pl.BlockSpec((1, 1, q_max), lambda t, npc: (t, 0, 0), memory_space=pltpu.SMEM),
            pl.BlockSpec((1, 8, td), lambda t, npc: (t, 0, 0)),
            pl.BlockSpec((td, D_MODEL), lambda t, npc: (t, 0)),
            pl.BlockSpec(memory_space=pl.ANY),
        ],
        out_specs=pl.BlockSpec(memory_space=pl.ANY),
        scratch_shapes=[pltpu.VMEM((2, q_max * PIECE, D_MODEL // 2), jnp.int32),
                        pltpu.SemaphoreType.DMA((2,))],
    )
    return pl.pallas_call(
        _dispatch_kernel,
        grid_spec=grid_spec,
        out_shape=jax.ShapeDtypeStruct((n_slots, D_MODEL // 2), jnp.int32),
        input_output_aliases={4: 0},
        compiler_params=_cparams(("arbitrary",)),
        name="moe_dispatch",
    )(n_pieces, piece_slot, lp_rows, h2, init)


def _expert_kernel(be_ref, nact_ref, x_ref, wg_ref, wu_ref, wd_ref, y_ref, wgb, wub, wdb):
    i = pl.program_id(0)

    @pl.when(i < nact_ref[0])
    def _():
        prev = be_ref[jnp.maximum(i - 1, 0)]

        @pl.when((i == 0) | (be_ref[i] != prev))
        def _():
            wgb[...] = wg_ref[0, 0].astype(BF16)
            wub[...] = wu_ref[0, 0].astype(BF16)
            wdb[...] = wd_ref[0, 0].astype(BF16)

        half = D_MODEL // 2
        x_lo, x_hi = _unpack_bf16_pair(x_ref[...])
        gate = _dot(x_lo, wgb[0:half]) + _dot(x_hi, wgb[half:D_MODEL])
        up = _dot(x_lo, wub[0:half]) + _dot(x_hi, wub[half:D_MODEL])
        hid = gate * _sigmoid(gate) * up
        y = _dot(hid.astype(BF16), wdb[...])
        y_ref[...] = _pack_bf16_pair(y[:, 0:half], y[:, half:D_MODEL])

    @pl.when(i >= nact_ref[0])
    def _():
        y_ref[...] = jnp.zeros_like(y_ref)


def _expert_call(block_expert, n_active, x_slots, w_gate, w_up, w_down, layer):
    n_blocks = block_expert.shape[0]
    tm = MOE_TILE
    w_idx = lambda i, be, na: (layer, be[i], 0, 0)
    grid_spec = pltpu.PrefetchScalarGridSpec(
        num_scalar_prefetch=2,
        grid=(n_blocks,),
        in_specs=[
            pl.BlockSpec((tm, D_MODEL // 2), lambda i, be, na: (jnp.minimum(i, na[0] - 1), 0)),
            pl.BlockSpec((1, 1, D_MODEL, D_EXPERT), w_idx),
            pl.BlockSpec((1, 1, D_MODEL, D_EXPERT), w_idx),
            pl.BlockSpec((1, 1, D_EXPERT, D_MODEL), w_idx),
        ],
        out_specs=pl.BlockSpec((tm, D_MODEL // 2), lambda i, be, na: (i, 0)),
        scratch_shapes=[
            pltpu.VMEM((D_MODEL, D_EXPERT), BF16),
            pltpu.VMEM((D_MODEL, D_EXPERT), BF16),
            pltpu.VMEM((D_EXPERT, D_MODEL), BF16),
        ],
    )
    return pl.pallas_call(
        _expert_kernel,
        grid_spec=grid_spec,
        out_shape=jax.ShapeDtypeStruct((n_blocks * tm, D_MODEL // 2), jnp.int32),
        compiler_params=_cparams(("arbitrary",)),
        name="expert_mlp",
    )(block_expert, n_active, x_slots, w_gate, w_up, w_down)


def _combine_kernel(slot_ref, next_slot_ref, y_hbm, x_ref, route_ref, mod_ref, fg_ref, o_ref, ybuf, sem,
                    *, final):
    t = pl.program_id(0)
    n_q = slot_ref.shape[2]
    rows_l = ybuf.shape[1]

    def fetch(slots, buf):
        def issue(q, carry):
            dst = ybuf.at[buf, pl.ds(pl.multiple_of(q * PIECE, PIECE), PIECE)]
            slot = pl.multiple_of(slots[0, 0, q], PIECE)
            pltpu.make_async_copy(y_hbm.at[pl.ds(slot, PIECE)], dst, sem.at[buf]).start()
            return carry
        lax.fori_loop(0, n_q, issue, 0)

    cur = t % 2

    @pl.when(t == 0)
    def _():
        fetch(slot_ref, cur)

    @pl.when(t + 1 < pl.num_programs(0))
    def _():
        fetch(next_slot_ref, 1 - cur)

    pltpu.make_async_copy(y_hbm.at[pl.ds(0, rows_l)], ybuf.at[cur], sem.at[cur]).wait()

    rt = route_ref[...]
    pos = lax.broadcasted_iota(jnp.int32, (rt.shape[0], rows_l), 1)
    weights = jnp.zeros(pos.shape, F32)
    for k in range(TOP_K):
        lp_k = rt[:, 4 + k:5 + k].astype(jnp.int32)
        weights = jnp.where(pos == lp_k, rt[:, k:k + 1], weights)
    y_lo, y_hi = _unpack_bf16_pair(ybuf[cur])
    wb = weights.astype(BF16)
    y = jnp.concatenate([_dot(wb, y_lo), _dot(wb, y_hi)], axis=-1)
    x2 = x_ref[...] + mod_ref[0][5:6] * y
    if final:
        x2 = x2 * lax.rsqrt(jnp.mean(x2 * x2, axis=-1, keepdims=True) + EPS) * fg_ref[...]
    o_ref[...] = x2


def _combine_call(piece_slot, y_slots, x, route_lp, mod, fg, bsz, seq, final):
    n = bsz * seq
    n_tiles, _, q_max = piece_slot.shape
    tc = n // n_tiles
    nt = seq // tc
    return pl.pallas_call(
        functools.partial(_combine_kernel, final=final),
        grid=(n_tiles,),
        in_specs=[
            pl.BlockSpec((1, 1, q_max), lambda t: (t, 0, 0), memory_space=pltpu.SMEM),
            pl.BlockSpec((1, 1, q_max), lambda t: (jnp.minimum(t + 1, n_tiles - 1), 0, 0),
                         memory_space=pltpu.SMEM),
            pl.BlockSpec(memory_space=pl.ANY),
            pl.BlockSpec((tc, D_MODEL), lambda t: (t, 0)),
            pl.BlockSpec((tc, 8), lambda t: (t, 0)),
            pl.BlockSpec((1, ADA_CHUNKS, D_MODEL), lambda t: (t // nt, 0, 0)),
            pl.BlockSpec((1, D_MODEL), lambda t: (0, 0)),
        ],
        out_specs=pl.BlockSpec((tc, D_MODEL), lambda t: (t, 0)),
        out_shape=jax.ShapeDtypeStruct((n, D_MODEL), F32),
        scratch_shapes=[pltpu.VMEM((2, q_max * PIECE, D_MODEL // 2), jnp.int32),
                        pltpu.SemaphoreType.DMA((2,))],
        compiler_params=_cparams(("arbitrary",)),
        name="moe_combine",
    )(piece_slot, piece_slot, y_slots, x, route_lp, mod, fg)


def _t5_bucket(rel):
    nb = T5_BUCKETS // 2
    max_exact = nb // 2
    bucket = jnp.where(rel > 0, nb, 0)
    n = jnp.abs(rel)
    large = max_exact + (jnp.log(jnp.maximum(n, 1).astype(F32) / max_exact)
                         / math.log(T5_MAX_DIST / max_exact) * (nb - max_exact)).astype(jnp.int32)
    large = jnp.minimum(large, nb - 1)
    return bucket + jnp.where(n < max_exact, n, large)


def _t5_tiles(t5_table, tq, tk):
    assert tk >= T5_MAX_DIST and tk % tq == 0
    par = jnp.arange(tk // tq, dtype=jnp.int32)[:, None, None, None]
    back = jnp.arange(2, dtype=jnp.int32)[None, :, None, None]
    r = jnp.arange(tq, dtype=jnp.int32)[None, None, :, None]
    c = jnp.arange(tk, dtype=jnp.int32)[None, None, None, :]
    rel = c - back * tk - (r + par * tq)
    far = t5_table[_t5_bucket(jnp.array(-4 * T5_MAX_DIST, jnp.int32))]
    bucket = _t5_bucket(rel)[..., None]
    b = jnp.zeros(rel.shape + (t5_table.shape[1],), F32)
    for k in range(T5_BUCKETS):
        b = jnp.where(bucket == k, t5_table[k], b)
    b = b - far
    return (b.transpose(4, 0, 1, 2, 3) * LOG2E).astype(F32)


def _place(cols, width, offset):
    z = jnp.zeros((cols.shape[0], width), cols.dtype)
    return z.at[:, offset:offset + cols.shape[1]].set(cols)


def _rot_half(cols):
    half = cols.shape[1] // 2
    return jnp.concatenate([-cols[:, half:], cols[:, :half]], axis=1)


def _prep_in_proj(w_in):
    cq = _place(w_in[:, 0:MLA_Q_LORA], _CQ_PAD, 0)
    ckv = w_in[:, MLA_Q_LORA:MLA_Q_LORA + MLA_KV_LORA]
    kr = w_in[:, MLA_Q_LORA + MLA_KV_LORA:_O_FOX]
    kr_p = _place(kr, LANES, MLA_NOPE)
    kr_rot = _place(_rot_half(kr), LANES, MLA_NOPE)
    fox = w_in[:, _O_FOX:_O_FOX + 3 * GROUP_WIDTH]
    wf = w_in[:, _O_FOX + 3 * GROUP_WIDTH:_O_CONV]
    conv = w_in[:, _O_CONV:_O_DIFF]
    diff = w_in[:, _O_DIFF:_O_DIFF + 3 * GROUP_WIDTH]
    w = jnp.concatenate([cq, ckv, kr_p, kr_rot, fox, conv, diff], axis=1).astype(BF16)
    wf_t = _place(wf, 16, 0).T.astype(BF16)
    return w, wf_t


def _prep_mla(w_uq, w_ukv):
    hd = MLA_NOPE + MLA_ROPE
    plain, rot, k_nope, v = [], [], [], []
    for h in range(4):
        wh = w_uq[:, h * hd:(h + 1) * hd]
        plain.append(_place(wh, LANES, 0))
        rot.append(_place(_rot_half(wh[:, MLA_NOPE:]), LANES, MLA_NOPE))
        kv = w_ukv[:, h * 2 * HEAD_DIM:(h + 1) * 2 * HEAD_DIM]
        k_nope.append(_place(kv[:, :MLA_NOPE], LANES, 0))
        v.append(kv[:, MLA_NOPE:])
    wuq = jnp.concatenate(plain + rot, axis=1)
    wuq = jnp.concatenate([wuq, jnp.zeros((_CQ_PAD - MLA_Q_LORA, wuq.shape[1]), wuq.dtype)], axis=0)
    wukv = jnp.concatenate(k_nope + v, axis=1)
    return wuq.astype(BF16), wukv.astype(BF16)


def _rope_tables(positions):
    half = MLA_ROPE // 2
    inv_freq = ROPE_THETA ** (-jnp.arange(half, dtype=F32) / half)
    ang = positions.astype(F32).reshape(-1)[:, None] * inv_freq
    cos, sin = jnp.cos(ang), jnp.sin(ang)
    n = ang.shape[0]
    cos_t = jnp.ones((n, LANES), F32).at[:, MLA_NOPE:MLA_NOPE + MLA_ROPE].set(jnp.concatenate([cos, cos], 1))
    sin_t = jnp.zeros((n, LANES), F32).at[:, MLA_NOPE:MLA_NOPE + MLA_ROPE].set(jnp.concatenate([sin, sin], 1))
    return cos_t, sin_t


def _piece_rows(tile):
    rows = tile * TOP_K + N_EXPERTS * (PIECE - 1)
    return -(-rows // LANES) * LANES


def _dispatch(expert, n_tok, tile):
    tm = MOE_TILE
    n_assign = n_tok * TOP_K
    per_tile = tile * TOP_K
    n_tiles = n_tok // tile
    q_max = _piece_rows(tile) // PIECE
    chunk = LANES
    experts = jnp.arange(N_EXPERTS, dtype=jnp.int32)
    e_flat = expert.reshape(n_assign)
    onehot = e_flat[:, None] == experts[None, :]
    oh = onehot.astype(BF16).reshape(n_assign // chunk, chunk, N_EXPERTS)
    tri = (jnp.arange(chunk)[:, None] >= jnp.arange(chunk)[None, :]).astype(BF16)
    within = jnp.einsum("ij,cjk->cik", tri, oh, preferred_element_type=F32)
    total = within[:, chunk - 1, :]
    before = jnp.cumsum(total, axis=0) - total
    running = (within + before[:, None, :]).reshape(n_tiles, per_tile, N_EXPERTS)
    onehot = onehot.reshape(n_tiles, per_tile, N_EXPERTS)

    run_end = running[:, per_tile - 1, :]
    run_start = jnp.concatenate([jnp.zeros((1, N_EXPERTS), F32), run_end[:-1]], axis=0)
    pieces = ((run_end - run_start).astype(jnp.int32) + PIECE - 1) // PIECE
    run_rows = pieces * PIECE
    rows_before = jnp.cumsum(run_rows, axis=0) - run_rows
    counts = rows_before[-1] + run_rows[-1]
    padded = (counts + tm - 1) // tm * tm
    pad_ends = jnp.cumsum(padded)
    pad_starts = pad_ends - padded

    piece_end = jnp.cumsum(pieces, axis=1)
    piece_first = piece_end - pieces
    local_start = (piece_first * PIECE).astype(F32)
    local = running - 1.0 - run_start[:, None, :] + local_start[:, None, :]
    lp = jnp.sum(jnp.where(onehot, local, 0.0), axis=2).astype(jnp.int32)
    lp = lp.reshape(n_tok, TOP_K)

    q = jnp.arange(q_max, dtype=jnp.int32)[None, :]
    e_of_q = jnp.sum((piece_end[:, None, :] <= q[:, :, None]).astype(jnp.int32), axis=2)
    pick = e_of_q[:, :, None] == experts[None, None, :]
    run_slot = pad_starts[None, :] + rows_before
    first_q = jnp.sum(jnp.where(pick, piece_first[:, None, :], 0), axis=2)
    slot_q = jnp.sum(jnp.where(pick, run_slot[:, None, :], 0), axis=2)
    piece_slot = jnp.where(e_of_q < N_EXPERTS, slot_q + (q - first_q) * PIECE, 0)
    n_pieces = piece_end[:, N_EXPERTS - 1]

    n_blocks = -(-(n_assign + N_EXPERTS * (n_tiles * (PIECE - 1) + tm - 1)) // tm)
    block_start = jnp.arange(n_blocks, dtype=jnp.int32) * tm
    block_expert = jnp.minimum(
        jnp.sum((pad_ends[None, :] <= block_start[:, None]).astype(jnp.int32), axis=1), N_EXPERTS - 1)
    n_active = (pad_ends[-1] // tm).astype(jnp.int32).reshape(1)
    return (block_expert, n_active, lp, piece_slot.reshape(n_tiles, 1, q_max).astype(jnp.int32),
            n_pieces.astype(jnp.int32), n_blocks * tm)


def kernel(x, c, positions, t5_table, ada_w, ada_b, norm_mix_g, norm_ffn_g, w_in, mla_q_norm_g, mla_w_uq, mla_kv_norm_g, mla_w_ukv, fox_forget_b, conv_w, diff_lambda, diff_subln_g, w_out, router_group_w, router_group_b, router_expert_w, router_expert_b, expert_w_gate, expert_w_up, expert_w_down, final_norm_g):
    bsz, seq, d = x.shape
    n = bsz * seq
    tq = min(ATT_Q, seq)
    tk = min(ATT_K, seq)
    tc = min(CMB_TILE, seq)

    mods = _ada_call(c, ada_w, ada_b).reshape(DEPTH, bsz, ADA_CHUNKS, D_MODEL)
    cos_t, sin_t = _rope_tables(positions)
    bias_tiles = _t5_tiles(t5_table, tq, tk)
    xf = x.reshape(n, d)

    for layer in range(DEPTH):
        mod = mods[layer]
        w, wf_t = _prep_in_proj(w_in[layer])
        wuq, wukv = _prep_mla(mla_w_uq[layer], mla_w_ukv[layer])
        fb = _place(fox_forget_b[layer][None, :], 16, 0).T.astype(F32)
        qg = _place(mla_q_norm_g[layer][None, :], _CQ_PAD, 0)
        kvg = mla_kv_norm_g[layer][None, :]
        cw = _place(conv_w[layer].T, 8, 0).T
        (qa, ka, va, qb, kb, vb, nf, oc, qd, kd, vd) = _proj_call(
            xf, mod, norm_mix_g[layer][None, :], w, wf_t, fb, wuq, wukv, qg, kvg, cos_t, sin_t, cw, bsz, seq)

        nf_t = nf.reshape(bsz, 8, seq // tk, tk).transpose(0, 2, 1, 3)
        lam_init = 0.8 - 0.6 * math.exp(-0.3 * layer)
        lam_p = diff_lambda[layer].astype(F32)
        lam = jnp.exp(jnp.sum(lam_p[0] * lam_p[1])) - jnp.exp(jnp.sum(lam_p[2] * lam_p[3])) + lam_init
        lam_row = jnp.full((1, LANES), lam, F32)
        sg = (jnp.concatenate([diff_subln_g[layer]] * 2) * (1.0 - lam_init))[None, :].astype(F32)
        oa, ob, od = _attn_call(
            [("A", (qa, ka, va)), ("B", (qb, kb, vb, nf_t)), ("D", (qd, kd, vd, bias_tiles, lam_row, sg))],
            bsz, seq)

        wr = jnp.concatenate([router_group_w[layer], router_expert_w[layer]], axis=1)
        wr_t = _place(wr, 80, 0).T
        wr_hi = wr_t.astype(BF16)
        wr_lo = (wr_t - wr_hi.astype(F32)).astype(BF16)
        rb = _place(jnp.concatenate([router_group_b[layer], router_expert_b[layer]])[None, :], 80, 0).T
        x1, h2, route = _out_router_call(
            oa, ob, oc, od, w_out[layer].astype(BF16), xf, mod, norm_ffn_g[layer][None, :],
            wr_hi, wr_lo, rb.astype(F32), bsz, seq)

        route_t = route.T
        expert = route_t[:, 2:4].astype(jnp.int32)
        block_expert, n_active, lp, piece_slot, n_pieces, n_slots = _dispatch(expert, n, tc)
        lp_rows = jnp.full((n // tc, 8, tc), -1, jnp.int32).at[:, 0:TOP_K, :].set(
            lp.reshape(n // tc, tc, TOP_K).transpose(0, 2, 1))
        x_slots = _dispatch_call(n_pieces, piece_slot, lp_rows, h2, n_slots)
        y_slots = _expert_call(block_expert, n_active, x_slots,
                               expert_w_gate, expert_w_up, expert_w_down, layer)
        route_lp = jnp.concatenate([route_t[:, 0:4], lp.astype(F32), route_t[:, 6:8]], axis=1)
        xf = _combine_call(piece_slot, y_slots, x1, route_lp, mod, final_norm_g[None, :], bsz, seq,
                           final=(layer == DEPTH - 1))
    return xf.reshape(bsz, seq, d)
```

```python
import functools
import math

import jax
import jax.numpy as jnp
from jax import lax
from jax.experimental import pallas as pl
from jax.experimental.pallas import tpu as pltpu

F32 = jnp.float32
BF16 = jnp.bfloat16

D_MODEL = 1024
DEPTH = 4
CHUNK = 64
EPS = 1e-6
HEAD_DIM = 64
GROUP_WIDTH = 256
MLA_NOPE = 64
MLA_ROPE = 32
MLA_Q_LORA = 192
MLA_KV_LORA = 128
ROPE_THETA = 10000.0
DIFF_HALF = 32
T5_BUCKETS = 32
T5_MAX_DIST = 128
N_GROUPS = 8
EXPERTS_PER_GROUP = 8
N_EXPERTS = 64
TOP_K = 2
D_EXPERT = 512
ADA_CHUNKS = 6

LANES = 128
LOG2E = 1.4426950408889634
NEG = -1e30

_O_MLA = 0
_O_FOX = MLA_Q_LORA + MLA_KV_LORA + MLA_ROPE
_O_CONV = _O_FOX + 3 * GROUP_WIDTH + 4
_O_DIFF = _O_CONV + 3 * GROUP_WIDTH

_CQ_PAD = 256
_W_A = _CQ_PAD + MLA_KV_LORA + 2 * LANES
_W_B = 3 * GROUP_WIDTH
_W_C = 3 * GROUP_WIDTH
_W_D = 3 * GROUP_WIDTH
_P_TOTAL = _W_A + _W_B + _W_C + _W_D

ROW_TILE = 512
ATT_Q = 256
ATT_K = 256
MOE_TILE = 1024
CMB_TILE = 512
PIECE = 8
VMEM_LIMIT = 56 * 1024 * 1024


def _cparams(sem):
    return pltpu.CompilerParams(dimension_semantics=sem, vmem_limit_bytes=VMEM_LIMIT)


def _split_bf16(a):
    hi = a.astype(BF16)
    lo = (a - hi.astype(F32)).astype(BF16)
    return hi, lo


def _dot(a, b):
    return jnp.dot(a, b, preferred_element_type=F32)


def _dot_nt(a, b):
    return lax.dot_general(a, b, (((1,), (1,)), ((), ())), preferred_element_type=F32)


def _sigmoid(z):
    return 1.0 / (1.0 + jnp.exp(-z))


_HIGH16 = -65536


def _pack_bf16_pair(lo, hi):
    lo_bits = pltpu.bitcast(lo.astype(BF16).astype(F32), jnp.int32)
    hi_bits = pltpu.bitcast(hi.astype(BF16).astype(F32), jnp.int32)
    sixteen = jnp.full(lo_bits.shape, 16, jnp.int32)
    return (hi_bits & _HIGH16) | lax.shift_right_logical(lo_bits, sixteen)


def _unpack_bf16_pair(packed):
    lo = pltpu.bitcast(lax.shift_left(packed, jnp.full(packed.shape, 16, jnp.int32)), F32)
    hi = pltpu.bitcast(packed & _HIGH16, F32)
    return lo.astype(BF16), hi.astype(BF16)


def _ada_kernel(c_ref, w_ref, b_ref, o_ref):
    c = c_ref[...]
    cond = c * _sigmoid(c)
    c_hi, c_lo = _split_bf16(cond)
    w_hi, w_lo = _split_bf16(w_ref[0])
    o_ref[0] = _dot(c_hi, w_hi) + _dot(c_hi, w_lo) + _dot(c_lo, w_hi) + b_ref[0]


def _ada_call(c, ada_w, ada_b):
    bsz = c.shape[0]
    n_col = ADA_CHUNKS * D_MODEL // D_MODEL
    return pl.pallas_call(
        _ada_kernel,
        grid=(DEPTH, n_col),
        in_specs=[
            pl.BlockSpec((bsz, D_MODEL), lambda l, j: (0, 0)),
            pl.BlockSpec((1, D_MODEL, D_MODEL), lambda l, j: (l, 0, j)),
            pl.BlockSpec((1, 1, D_MODEL), lambda l, j: (l, 0, j)),
        ],
        out_specs=pl.BlockSpec((1, bsz, D_MODEL), lambda l, j: (l, 0, j)),
        out_shape=jax.ShapeDtypeStruct((DEPTH, bsz, ADA_CHUNKS * D_MODEL), F32),
        compiler_params=_cparams(("arbitrary", "arbitrary")),
        name="ada_mod",
    )(c, ada_w, ada_b.reshape(DEPTH, 1, ADA_CHUNKS * D_MODEL))


def _values_with_ones(v):
    lane = lax.broadcasted_iota(jnp.int32, (1, LANES), 1)
    low_half = lane < HEAD_DIM
    slabs = []
    for head in range(4):
        pair = v[:, (head // 2) * LANES:(head // 2 + 1) * LANES]
        keep = low_half if head % 2 == 0 else jnp.logical_not(low_half)
        slabs.append(jnp.where(keep, pair, 1.0))
    return jnp.concatenate(slabs, axis=-1).astype(BF16)


def _proj_kernel(x_ref, mod_ref, g_ref, w_ref, wf_ref, fb_ref, wuq_ref, wukv_ref, qg_ref, kvg_ref,
                 cos_ref, sin_ref, cw_ref,
                 qa_ref, ka_ref, va_ref, qb_ref, kb_ref, vb_ref, nf_ref, oc_ref, qd_ref, kd_ref, vd_ref,
                 zc_ref, fc_ref, *, scale_a, scale_b, scale_d):
    t = pl.program_id(1)
    rows = x_ref.shape[0]

    @pl.when(t == 0)
    def _():
        zc_ref[...] = jnp.zeros_like(zc_ref)
        fc_ref[...] = jnp.zeros_like(fc_ref)

    x = x_ref[...]
    mod = mod_ref[0]
    hn = x * lax.rsqrt(jnp.mean(x * x, axis=-1, keepdims=True) + EPS) * g_ref[...]
    h = hn * (1.0 + mod[1:2]) + mod[0:1]
    hb = h.astype(BF16)

    z = _dot_nt(wf_ref[...], hb) + fb_ref[...]
    log_f = jnp.minimum(z, 0.0) - jnp.log(1.0 + jnp.exp(-jnp.abs(z)))
    r_i = lax.broadcasted_iota(jnp.int32, (rows, rows), 0)
    c_i = lax.broadcasted_iota(jnp.int32, (rows, rows), 1)
    tri = jnp.where(r_i <= c_i, 1.0, 0.0).astype(BF16)
    f_hi = log_f.astype(BF16)
    rem = log_f - f_hi.astype(F32)
    f_mid = rem.astype(BF16)
    f_lo = (rem - f_mid.astype(F32)).astype(BF16)
    cum = _dot(f_hi, tri) + _dot(f_mid, tri) + _dot(f_lo, tri) + fc_ref[:, 0:1]
    fc_ref[...] = jnp.broadcast_to(cum[:, rows - 1:rows], fc_ref.shape)
    nf_ref[0] = cum[0:8] * (-LOG2E)

    pa = _dot(hb, w_ref[:, 0:_W_A])
    cq = pa[:, 0:_CQ_PAD]
    ckv = pa[:, _CQ_PAD:_CQ_PAD + MLA_KV_LORA]
    kr = pa[:, _CQ_PAD + MLA_KV_LORA:_CQ_PAD + MLA_KV_LORA + LANES]
    krr = pa[:, _CQ_PAD + MLA_KV_LORA + LANES:_W_A]
    cqn = cq * lax.rsqrt(jnp.sum(cq * cq, axis=-1, keepdims=True) * (1.0 / MLA_Q_LORA) + EPS) * qg_ref[...]
    ckvn = ckv * lax.rsqrt(jnp.mean(ckv * ckv, axis=-1, keepdims=True) + EPS) * kvg_ref[...]
    q2 = _dot(cqn.astype(BF16), wuq_ref[...])
    kv2 = _dot(ckvn.astype(BF16), wukv_ref[...])
    cos = cos_ref[...]
    sin = sin_ref[...]
    cos4 = jnp.concatenate([cos] * 4, axis=-1)
    sin4 = jnp.concatenate([sin] * 4, axis=-1)
    qa = (q2[:, 0:512] * cos4 + q2[:, 512:1024] * sin4) * (scale_a * LOG2E)
    k_rope = kr * cos + krr * sin
    ka = kv2[:, 0:512] + jnp.concatenate([k_rope] * 4, axis=-1)
    qa_ref[...] = qa.astype(BF16)
    ka_ref[...] = ka.astype(BF16)
    va_ref[...] = _values_with_ones(kv2[:, 512:768])

    pb = _dot(hb, w_ref[:, _W_A:_W_A + _W_B])
    qb_ref[...] = (pb[:, 0:256] * (scale_b * LOG2E)).astype(BF16)
    kb_ref[...] = pb[:, 256:512].astype(BF16)
    vb_ref[...] = _values_with_ones(pb[:, 512:768])

    pc = _dot(hb, w_ref[:, _W_A + _W_B:_W_A + _W_B + _W_C])
    zz = pc[:, 256:512] * pc[:, 512:768]
    ext = jnp.concatenate([zc_ref[...], zz], axis=0)
    cw = cw_ref[...]
    conv = zz * cw[2:3] + ext[7:rows + 7] * cw[1:2] + ext[6:rows + 6] * cw[0:1]
    zc_ref[...] = zz[rows - 8:rows]
    oc_ref[...] = (pc[:, 0:256] * conv).astype(BF16)

    pd = _dot(hb, w_ref[:, _W_A + _W_B + _W_C:_P_TOTAL])
    qd_ref[...] = (pd[:, 0:256] * (scale_d * LOG2E)).astype(BF16)
    kd_ref[...] = pd[:, 256:512].astype(BF16)
    vd_ref[...] = _values_with_ones(pd[:, 512:768])


def _proj_call(x, mod, g, w, wf, fb, wuq, wukv, qg, kvg, cos_t, sin_t, cw, bsz, seq):
    n = bsz * seq
    rt = min(ROW_TILE, seq)
    nt = seq // rt
    row = lambda width: pl.BlockSpec((rt, width), lambda b, t: (b * nt + t, 0))
    full = lambda a: pl.BlockSpec(a.shape, lambda b, t: (0,) * a.ndim)
    bf = lambda width: jax.ShapeDtypeStruct((n, width), BF16)
    kern = functools.partial(
        _proj_kernel,
        scale_a=(MLA_NOPE + MLA_ROPE) ** -0.5, scale_b=HEAD_DIM ** -0.5, scale_d=DIFF_HALF ** -0.5)
    return pl.pallas_call(
        kern,
        grid=(bsz, nt),
        in_specs=[
            row(D_MODEL),
            pl.BlockSpec((1, ADA_CHUNKS, D_MODEL), lambda b, t: (b, 0, 0)),
            full(g), full(w), full(wf), full(fb), full(wuq), full(wukv), full(qg), full(kvg),
            row(LANES), row(LANES), full(cw),
        ],
        out_specs=[
            row(512), row(512), row(512), row(256), row(256), row(512),
            pl.BlockSpec((1, 8, rt), lambda b, t: (b, 0, t)),
            row(256), row(256), row(256), row(512),
        ],
        out_shape=[
            bf(512), bf(512), bf(512), bf(256), bf(256), bf(512),
            jax.ShapeDtypeStruct((bsz, 8, seq), F32),
            bf(256), bf(256), bf(256), bf(512),
        ],
        scratch_shapes=[pltpu.VMEM((8, GROUP_WIDTH), F32), pltpu.VMEM((16, LANES), F32)],
        compiler_params=_cparams(("arbitrary", "arbitrary")),
        name="norm_in_proj",
    )(x, mod, g, w, wf, fb, wuq, wukv, qg, kvg, cos_t, sin_t, cw)


def _tile_update(q, k_t, v_t, m_ref, acc_ref, bias=None, mask=None):
    s = _dot_nt(q, k_t)
    if bias is not None:
        s = s + bias
    if mask is not None:
        s = jnp.where(mask, s, NEG)
    m_prev = m_ref[...]
    m_new = jnp.maximum(m_prev, jnp.max(s, axis=-1, keepdims=True))
    alpha = jnp.exp2(m_prev - m_new)
    p = jnp.exp2(s - jnp.concatenate([m_new] * (s.shape[1] // LANES), axis=-1))
    acc_ref[...] = alpha * acc_ref[...] + _dot(p.astype(BF16), v_t)
    m_ref[...] = m_new


_ATTN_INPUTS = {"A": 3, "B": 4, "D": 6}
_ATTN_MAPS = {"A": 4, "B": 4, "D": 8}


def _attn_kernel(*refs, kinds, tq, tk):
    pos = 0
    ins, outs, q_scrs, m_refs, acc_refs = {}, {}, {}, {}, {}
    for kd in kinds:
        ins[kd] = refs[pos:pos + _ATTN_INPUTS[kd]]
        pos += _ATTN_INPUTS[kd]
    for kd in kinds:
        outs[kd] = refs[pos]
        pos += 1
    for kd in kinds:
        nm = _ATTN_MAPS[kd]
        q_scrs[kd] = refs[pos]
        m_refs[kd] = refs[pos + 1:pos + 1 + nm]
        acc_refs[kd] = refs[pos + 1 + nm:pos + 1 + 2 * nm]
        pos += 1 + 2 * nm
    i = pl.program_id(1)
    ratio = tk // tq
    jd = i // ratio
    par = i % ratio
    off = par * tq
    lane = lax.broadcasted_iota(jnp.int32, (1, LANES), 1)
    low_half = lane < HEAD_DIM
    r_i = lax.broadcasted_iota(jnp.int32, (tq, tk), 0) + off
    c_i = lax.broadcasted_iota(jnp.int32, (tq, tk), 1)
    shift = CHUNK.bit_length() - 1
    diag_masks = {
        "frame": c_i <= r_i,
        "chunk": jnp.right_shift(c_i, shift) <= jnp.right_shift(r_i, shift),
    }

    maps = []
    for kd in kinds:
        q_ref, q_scr = ins[kd][0], q_scrs[kd]
        for head in range(4):
            slab = head // 2
            if kd == "A":
                q_scr[head] = q_ref[:, head * LANES:(head + 1) * LANES]
                maps.append((kd, head, head, head))
            elif kd == "B":
                q = q_ref[:, slab * LANES:(slab + 1) * LANES]
                sel = low_half if head % 2 == 0 else jnp.logical_not(low_half)
                q_scr[head] = jnp.where(sel, q, jnp.zeros_like(q))
                maps.append((kd, head, slab, head))
            else:
                q = q_ref[:, slab * LANES:(slab + 1) * LANES]
                for mp in range(2):
                    lo = (2 * (head % 2) + mp) * DIFF_HALF
                    sel = (lane >= lo) & (lane < lo + DIFF_HALF)
                    q_scr[2 * head + mp] = jnp.where(sel, q, jnp.zeros_like(q))
                    maps.append((kd, 2 * head + mp, slab, head))
        for idx in range(_ATTN_MAPS[kd]):
            m_refs[kd][idx][...] = jnp.full(m_refs[kd][idx].shape, NEG, F32)
            acc_refs[kd][idx][...] = jnp.zeros(acc_refs[kd][idx].shape, F32)

    def step(j, mode):
        start = pl.multiple_of(j * tk, tk)
        for kd, idx, k_slab, head in maps:
            k_t = ins[kd][1][pl.ds(start, tk), k_slab * LANES:(k_slab + 1) * LANES]
            v_t = ins[kd][2][pl.ds(start, tk), head * LANES:(head + 1) * LANES]
            bias = None
            if kd == "B":
                bias = ins[kd][3][0, j][head:head + 1, :]
            elif kd == "D" and mode != "far":
                bias = ins[kd][3][head, par, 0 if mode == "diag" else 1]
            mask = diag_masks["frame" if kd == "B" else "chunk"] if mode == "diag" else None
            _tile_update(q_scrs[kd][idx], k_t, v_t, m_refs[kd][idx], acc_refs[kd][idx],
                         bias=bias, mask=mask)

    def far_pair(jj, carry):
        step(2 * jj, "far")
        step(2 * jj + 1, "far")
        return carry

    n_far = jnp.maximum(jd - 1, 0)
    lax.fori_loop(0, n_far // 2, far_pair, 0)

    @pl.when(n_far % 2 == 1)
    def _():
        step(n_far - 1, "far")

    @pl.when(jd >= 1)
    def _():
        step(jd - 1, "prev")
        step(jd, "diag")

    @pl.when(jd == 0)
    def _():
        step(jd, "diag")

    def normalized(kd, idx):
        acc = acc_refs[kd][idx][...]
        return acc / pltpu.roll(acc, HEAD_DIM, axis=1)

    for kd in kinds:
        for slab in range(2):
            heads = []
            for sub in range(2):
                head = 2 * slab + sub
                if kd == "D":
                    lam = ins[kd][4][...]
                    heads.append(normalized(kd, 2 * head) - lam * normalized(kd, 2 * head + 1))
                else:
                    heads.append(normalized(kd, head))
            o_slab = jnp.where(low_half, heads[0], heads[1])
            if kd == "D":
                sq = o_slab * o_slab
                s_lo = jnp.sum(jnp.where(low_half, sq, 0.0), axis=-1, keepdims=True)
                s_hi = jnp.sum(jnp.where(low_half, 0.0, sq), axis=-1, keepdims=True)
                ms = jnp.where(low_half, s_lo, s_hi) * (1.0 / HEAD_DIM)
                o_slab = o_slab * lax.rsqrt(ms + EPS) * ins[kd][5][...]
            outs[kd][:, slab * LANES:(slab + 1) * LANES] = o_slab.astype(BF16)


def _attn_call(groups, bsz, seq):
    n = bsz * seq
    tq = min(ATT_Q, seq)
    tk = min(ATT_K, seq)
    nq = seq // tq
    kinds = tuple(kd for kd, _ in groups)
    operands, in_specs, scratch = [], [], []
    for kd, (q, k, v, *extra) in groups:
        operands += [q, k, v, *extra]
        in_specs += [
            pl.BlockSpec((tq, q.shape[1]), lambda b, i: (b * nq + i, 0)),
            pl.BlockSpec((seq, k.shape[1]), lambda b, i: (b, 0)),
            pl.BlockSpec((seq, v.shape[1]), lambda b, i: (b, 0)),
        ]
        if kd == "B":
            in_specs.append(pl.BlockSpec((1, seq // tk, 8, tk), lambda b, i: (b, 0, 0, 0)))
        else:
            in_specs += [pl.BlockSpec(e.shape, lambda b, i, nd=e.ndim: (0,) * nd) for e in extra]
        scratch += [pltpu.VMEM((_ATTN_MAPS[kd], tq, LANES), BF16)]
        scratch += [pltpu.VMEM((tq, LANES), F32) for _ in range(2 * _ATTN_MAPS[kd])]
    out_spec = pl.BlockSpec((tq, GROUP_WIDTH), lambda b, i: (b * nq + i, 0))
    return pl.pallas_call(
        functools.partial(_attn_kernel, kinds=kinds, tq=tq, tk=tk),
        grid=(bsz, nq),
        in_specs=in_specs,
        out_specs=[out_spec] * len(kinds),
        out_shape=[jax.ShapeDtypeStruct((n, GROUP_WIDTH), BF16)] * len(kinds),
        scratch_shapes=scratch,
        compiler_params=_cparams(("arbitrary", "arbitrary")),
        name="attn_" + "".join(kinds),
    )(*operands)


def _out_router_kernel(oa_ref, ob_ref, oc_ref, od_ref, wo_ref, x_ref, mod_ref, g_ref, wr_hi_ref, wr_lo_ref,
                       rb_ref, xo_ref, h2_ref, route_ref):
    gw = GROUP_WIDTH
    mix = (_dot(oa_ref[...], wo_ref[0:gw]) + _dot(ob_ref[...], wo_ref[gw:2 * gw])
           + _dot(oc_ref[...], wo_ref[2 * gw:3 * gw]) + _dot(od_ref[...], wo_ref[3 * gw:4 * gw]))
    mod = mod_ref[0]
    x1 = x_ref[...] + mod[2:3] * mix
    xo_ref[...] = x1
    hn = x1 * lax.rsqrt(jnp.mean(x1 * x1, axis=-1, keepdims=True) + EPS) * g_ref[...]
    h2 = hn * (1.0 + mod[4:5]) + mod[3:4]
    h2_ref[...] = h2.astype(BF16)

    a_hi, a_lo = _split_bf16(h2)
    w_hi = wr_hi_ref[...]
    lg = _dot_nt(w_hi, a_hi) + _dot_nt(w_hi, a_lo) + _dot_nt(wr_lo_ref[...], a_hi) + rb_ref[...]
    gl = lg[0:N_GROUPS]
    rows = gl.shape[1]
    iota8 = lax.broadcasted_iota(jnp.int32, (N_GROUPS, rows), 0)
    gmax = jnp.max(gl, axis=0, keepdims=True)
    grp = jnp.min(jnp.where(gl == gmax, iota8, N_GROUPS), axis=0, keepdims=True)
    p_grp = 1.0 / jnp.sum(jnp.exp(gl - gmax), axis=0, keepdims=True)
    esel = jnp.zeros((EXPERTS_PER_GROUP, rows), F32)
    for gi in range(N_GROUPS):
        lo = N_GROUPS + gi * EXPERTS_PER_GROUP
        esel = jnp.where(grp == gi, lg[lo:lo + EXPERTS_PER_GROUP], esel)
    ee = jnp.exp(esel - jnp.max(esel, axis=0, keepdims=True))
    ps = ee / jnp.sum(ee, axis=0, keepdims=True)
    p1 = jnp.max(ps, axis=0, keepdims=True)
    i1 = jnp.min(jnp.where(ps == p1, iota8, EXPERTS_PER_GROUP), axis=0, keepdims=True)
    ps2 = jnp.where(iota8 == i1, -1.0, ps)
    p2 = jnp.max(ps2, axis=0, keepdims=True)
    i2 = jnp.min(jnp.where(ps2 == p2, iota8, EXPERTS_PER_GROUP), axis=0, keepdims=True)
    den = p1 + p2
    g1 = p_grp * p1 / den
    g2 = p_grp * p2 / den
    e1 = (grp * EXPERTS_PER_GROUP + i1).astype(F32)
    e2 = (grp * EXPERTS_PER_GROUP + i2).astype(F32)
    zero = jnp.zeros_like(g1)
    route_ref[...] = jnp.concatenate([g1, g2, e1, e2, zero, zero, zero, zero], axis=0)


def _out_router_call(oa, ob, oc, od, wo, x, mod, g, wr_hi, wr_lo, rb, bsz, seq):
    n = bsz * seq
    rt = min(ROW_TILE, seq)
    nt = seq // rt
    row = lambda width: pl.BlockSpec((rt, width), lambda t: (t, 0))
    full = lambda a: pl.BlockSpec(a.shape, lambda t: (0,) * a.ndim)
    return pl.pallas_call(
        _out_router_kernel,
        grid=(n // rt,),
        in_specs=[
            row(GROUP_WIDTH), row(GROUP_WIDTH), row(GROUP_WIDTH), row(GROUP_WIDTH), full(wo),
            row(D_MODEL),
            pl.BlockSpec((1, ADA_CHUNKS, D_MODEL), lambda t: (t // nt, 0, 0)),
            full(g), full(wr_hi), full(wr_lo), full(rb),
        ],
        out_specs=[row(D_MODEL), row(D_MODEL), pl.BlockSpec((8, rt), lambda t: (0, t))],
        out_shape=[
            jax.ShapeDtypeStruct((n, D_MODEL), F32),
            jax.ShapeDtypeStruct((n, D_MODEL), BF16),
            jax.ShapeDtypeStruct((8, n), F32),
        ],
        compiler_params=_cparams(("arbitrary",)),
        name="out_proj_router",
    )(oa, ob, oc, od, wo, x, mod, g, wr_hi, wr_lo, rb)


def _dispatch_kernel(np_ref, slot_ref, lp_ref, h_ref, init_hbm, xs_hbm, sorted_buf, sem):
    del init_hbm
    t = pl.program_id(0)
    cur = t % 2
    rows_l = sorted_buf.shape[1]
    lp = lp_ref[0]
    pos = lax.broadcasted_iota(jnp.int32, (rows_l, lp.shape[1]), 0)
    hit = pos == lp[0:1, :]
    for k in range(1, TOP_K):
        hit = hit | (pos == lp[k:k + 1, :])
    perm = jnp.where(hit, 1.0, 0.0).astype(BF16)
    ordered = _dot(perm, h_ref[...])
    sorted_buf[cur] = _pack_bf16_pair(ordered[:, 0:D_MODEL // 2], ordered[:, D_MODEL // 2:D_MODEL])

    def piece(q, buf):
        src = sorted_buf.at[buf, pl.ds(pl.multiple_of(q * PIECE, PIECE), PIECE)]
        slot = pl.multiple_of(slot_ref[0, 0, q], PIECE)
        return pltpu.make_async_copy(src, xs_hbm.at[pl.ds(slot, PIECE)], sem.at[buf])

    def drain(count, buf):
        def body(q, carry):
            piece(q, buf).wait()
            return carry
        lax.fori_loop(0, count, body, 0)

    @pl.when(t > 0)
    def _():
        drain(np_ref[jnp.maximum(t - 1, 0)], 1 - cur)

    def issue(q, carry):
        piece(q, cur).start()
        return carry
    lax.fori_loop(0, np_ref[t], issue, 0)

    @pl.when(t + 1 == pl.num_programs(0))
    def _():
        drain(np_ref[t], cur)


def _dispatch_call(n_pieces, piece_slot, lp_rows, h2, n_slots):
    n = h2.shape[0]
    n_tiles, _, q_max = piece_slot.shape
    td = n // n_tiles
    init = jnp.zeros((n_slots, D_MODEL // 2), jnp.int32)
    grid_spec = pltpu.PrefetchScalarGridSpec(
        num_scalar_prefetch=1,
        grid=(n_tiles,),
        in_specs=[
            pl.BlockSpec((1, 1, q_max), lambda t, npc: (t, 0, 0), memory_space=pltpu.SMEM),
            pl.BlockSpec((1, 8, td), lambda t, npc: (t, 0, 0)),
            pl.BlockSpec((td, D_MODEL), lambda t, npc: (t, 0)),
            pl.BlockSpec(memory_space=pl.ANY),
        ],
        out_specs=pl.BlockSpec(memory_space=pl.ANY),
        scratch_shapes=[pltpu.VMEM((2, q_max * PIECE, D_MODEL // 2), jnp.int32),
                        pltpu.SemaphoreType.DMA((2,))],
    )
    return pl.pallas_call(
        _dispatch_kernel,
        grid_spec=grid_spec,
        out_shape=jax.ShapeDtypeStruct((n_slots, D_MODEL // 2), jnp.int32),
        input_output_aliases={4: 0},
        compiler_params=_cparams(("arbitrary",)),
        name="moe_dispatch",
    )(n_pieces, piece_slot, lp_rows, h2, init)


def _expert_kernel(be_ref, nact_ref, x_ref, wg_ref, wu_ref, wd_ref, y_ref, wgb, wub, wdb):
    i = pl.program_id(0)

    @pl.when(i < nact_ref[0])
    def _():
        prev = be_ref[jnp.maximum(i - 1, 0)]

        @pl.when((i == 0) | (be_ref[i] != prev))
        def _():
            wgb[...] = wg_ref[0, 0].astype(BF16)
            wub[...] = wu_ref[0, 0].astype(BF16)
            wdb[...] = wd_ref[0, 0].astype(BF16)

        half = D_MODEL // 2
        x_lo, x_hi = _unpack_bf16_pair(x_ref[...])
        gate = _dot(x_lo, wgb[0:half]) + _dot(x_hi, wgb[half:D_MODEL])
        up = _dot(x_lo, wub[0:half]) + _dot(x_hi, wub[half:D_MODEL])
        hid = gate * _sigmoid(gate) * up
        y = _dot(hid.astype(BF16), wdb[...])
        y_ref[...] = _pack_bf16_pair(y[:, 0:half], y[:, half:D_MODEL])

    @pl.when(i >= nact_ref[0])
    def _():
        y_ref[...] = jnp.zeros_like(y_ref)


def _expert_call(block_expert, n_active, x_slots, w_gate, w_up, w_down, layer):
    n_blocks = block_expert.shape[0]
    tm = MOE_TILE
    w_idx = lambda i, be, na: (layer, be[i], 0, 0)
    grid_spec = pltpu.PrefetchScalarGridSpec(
        num_scalar_prefetch=2,
        grid=(n_blocks,),
        in_specs=[
            pl.BlockSpec((tm, D_MODEL // 2), lambda i, be, na: (jnp.minimum(i, na[0] - 1), 0)),
            pl.BlockSpec((1, 1, D_MODEL, D_EXPERT), w_idx),
            pl.BlockSpec((1, 1, D_MODEL, D_EXPERT), w_idx),
            pl.BlockSpec((1, 1, D_EXPERT, D_MODEL), w_idx),
        ],
        out_specs=pl.BlockSpec((tm, D_MODEL // 2), lambda i, be, na: (i, 0)),
        scratch_shapes=[
            pltpu.VMEM((D_MODEL, D_EXPERT), BF16),
            pltpu.VMEM((D_MODEL, D_EXPERT), BF16),
            pltpu.VMEM((D_EXPERT, D_MODEL), BF16),
        ],
    )
    return pl.pallas_call(
        _expert_kernel,
        grid_spec=grid_spec,
        out_shape=jax.ShapeDtypeStruct((n_blocks * tm, D_MODEL // 2), jnp.int32),
        compiler_params=_cparams(("arbitrary",)),
        name="expert_mlp",
    )(block_expert, n_active, x_slots, w_gate, w_up, w_down)


def _combine_kernel(slot_ref, next_slot_ref, y_hbm, x_ref, route_ref, mod_ref, fg_ref, o_ref, ybuf, sem,
                    *, final):
    t = pl.program_id(0)
    n_q = slot_ref.shape[2]
    rows_l = ybuf.shape[1]

    def fetch(slots, buf):
        def issue(q, carry):
            dst = ybuf.at[buf, pl.ds(pl.multiple_of(q * PIECE, PIECE), PIECE)]
            slot = pl.multiple_of(slots[0, 0, q], PIECE)
            pltpu.make_async_copy(y_hbm.at[pl.ds(slot, PIECE)], dst, sem.at[buf]).start()
            return carry
        lax.fori_loop(0, n_q, issue, 0)

    cur = t % 2

    @pl.when(t == 0)
    def _():
        fetch(slot_ref, cur)

    @pl.when(t + 1 < pl.num_programs(0))
    def _():
        fetch(next_slot_ref, 1 - cur)

    pltpu.make_async_copy(y_hbm.at[pl.ds(0, rows_l)], ybuf.at[cur], sem.at[cur]).wait()

    rt = route_ref[...]
    pos = lax.broadcasted_iota(jnp.int32, (rt.shape[0], rows_l), 1)
    weights = jnp.zeros(pos.shape, F32)
    for k in range(TOP_K):
        lp_k = rt[:, 4 + k:5 + k].astype(jnp.int32)
        weights = jnp.where(pos == lp_k, rt[:, k:k + 1], weights)
    y_lo, y_hi = _unpack_bf16_pair(ybuf[cur])
    wb = weights.astype(BF16)
    y = jnp.concatenate([_dot(wb, y_lo), _dot(wb, y_hi)], axis=-1)
    x2 = x_ref[...] + mod_ref[0][5:6] * y
    if final:
        x2 = x2 * lax.rsqrt(jnp.mean(x2 * x2, axis=-1, keepdims=True) + EPS) * fg_ref[...]
    o_ref[...] = x2


def _combine_call(piece_slot, y_slots, x, route_lp, mod, fg, bsz, seq, final):
    n = bsz * seq
    n_tiles, _, q_max = piece_slot.shape
    tc = n // n_tiles
    nt = seq // tc
    return pl.pallas_call(
        functools.partial(_combine_kernel, final=final),
        grid=(n_tiles,),
        in_specs=[
            pl.BlockSpec((1, 1, q_max), lambda t: (t, 0, 0), memory_space=pltpu.SMEM),
            pl.BlockSpec((1, 1, q_max), lambda t: (jnp.minimum(t + 1, n_tiles - 1), 0, 0),
                         memory_space=pltpu.SMEM),
            pl.BlockSpec(memory_space=pl.ANY),
            pl.BlockSpec((tc, D_MODEL), lambda t: (t, 0)),
            pl.BlockSpec((tc, 8), lambda t: (t, 0)),
            pl.BlockSpec((1, ADA_CHUNKS, D_MODEL), lambda t: (t // nt, 0, 0)),
            pl.BlockSpec((1, D_MODEL), lambda t: (0, 0)),
        ],
        out_specs=pl.BlockSpec((tc, D_MODEL), lambda t: (t, 0)),
        out_shape=jax.ShapeDtypeStruct((n, D_MODEL), F32),
        scratch_shapes=[pltpu.VMEM((2, q_max * PIECE, D_MODEL // 2), jnp.int32),
                        pltpu.SemaphoreType.DMA((2,))],
        compiler_params=_cparams(("arbitrary",)),
        name="moe_combine",
    )(piece_slot, piece_slot, y_slots, x, route_lp, mod, fg)


def _t5_bucket(rel):
    nb = T5_BUCKETS // 2
    max_exact = nb // 2
    bucket = jnp.where(rel > 0, nb, 0)
    n = jnp.abs(rel)
    large = max_exact + (jnp.log(jnp.maximum(n, 1).astype(F32) / max_exact)
                         / math.log(T5_MAX_DIST / max_exact) * (nb - max_exact)).astype(jnp.int32)
    large = jnp.minimum(large, nb - 1)
    return bucket + jnp.where(n < max_exact, n, large)


def _t5_tiles(t5_table, tq, tk):
    assert tk >= T5_MAX_DIST and tk % tq == 0
    par = jnp.arange(tk // tq, dtype=jnp.int32)[:, None, None, None]
    back = jnp.arange(2, dtype=jnp.int32)[None, :, None, None]
    r = jnp.arange(tq, dtype=jnp.int32)[None, None, :, None]
    c = jnp.arange(tk, dtype=jnp.int32)[None, None, None, :]
    rel = c - back * tk - (r + par * tq)
    far = t5_table[_t5_bucket(jnp.array(-4 * T5_MAX_DIST, jnp.int32))]
    bucket = _t5_bucket(rel)[..., None]
    b = jnp.zeros(rel.shape + (t5_table.shape[1],), F32)
    for k in range(T5_BUCKETS):
        b = jnp.where(bucket == k, t5_table[k], b)
    b = b - far
    return (b.transpose(4, 0, 1, 2, 3) * LOG2E).astype(F32)


def _place(cols, width, offset):
    z = jnp.zeros((cols.shape[0], width), cols.dtype)
    return z.at[:, offset:offset + cols.shape[1]].set(cols)


def _rot_half(cols):
    half = cols.shape[1] // 2
    return jnp.concatenate([-cols[:, half:], cols[:, :half]], axis=1)


def _prep_in_proj(w_in):
    cq = _place(w_in[:, 0:MLA_Q_LORA], _CQ_PAD, 0)
    ckv = w_in[:, MLA_Q_LORA:MLA_Q_LORA + MLA_KV_LORA]
    kr = w_in[:, MLA_Q_LORA + MLA_KV_LORA:_O_FOX]
    kr_p = _place(kr, LANES, MLA_NOPE)
    kr_rot = _place(_rot_half(kr), LANES, MLA_NOPE)
    fox = w_in[:, _O_FOX:_O_FOX + 3 * GROUP_WIDTH]
    wf = w_in[:, _O_FOX + 3 * GROUP_WIDTH:_O_CONV]
    conv = w_in[:, _O_CONV:_O_DIFF]
    diff = w_in[:, _O_DIFF:_O_DIFF + 3 * GROUP_WIDTH]
    w = jnp.concatenate([cq, ckv, kr_p, kr_rot, fox, conv, diff], axis=1).astype(BF16)
    wf_t = _place(wf, 16, 0).T.astype(BF16)
    return w, wf_t


def _prep_mla(w_uq, w_ukv):
    hd = MLA_NOPE + MLA_ROPE
    plain, rot, k_nope, v = [], [], [], []
    for h in range(4):
        wh = w_uq[:, h * hd:(h + 1) * hd]
        plain.append(_place(wh, LANES, 0))
        rot.append(_place(_rot_half(wh[:, MLA_NOPE:]), LANES, MLA_NOPE))
        kv = w_ukv[:, h * 2 * HEAD_DIM:(h + 1) * 2 * HEAD_DIM]
        k_nope.append(_place(kv[:, :MLA_NOPE], LANES, 0))
        v.append(kv[:, MLA_NOPE:])
    wuq = jnp.concatenate(plain + rot, axis=1)
    wuq = jnp.concatenate([wuq, jnp.zeros((_CQ_PAD - MLA_Q_LORA, wuq.shape[1]), wuq.dtype)], axis=0)
    wukv = jnp.concatenate(k_nope + v, axis=1)
    return wuq.astype(BF16), wukv.astype(BF16)


def _rope_tables(positions):
    half = MLA_ROPE // 2
    inv_freq = ROPE_THETA ** (-jnp.arange(half, dtype=F32) / half)
    ang = positions.astype(F32).reshape(-1)[:, None] * inv_freq
    cos, sin = jnp.cos(ang), jnp.sin(ang)
    n = ang.shape[0]
    cos_t = jnp.ones((n, LANES), F32).at[:, MLA_NOPE:MLA_NOPE + MLA_ROPE].set(jnp.concatenate([cos, cos], 1))
    sin_t = jnp.zeros((n, LANES), F32).at[:, MLA_NOPE:MLA_NOPE + MLA_ROPE].set(jnp.concatenate([sin, sin], 1))
    return cos_t, sin_t


def _piece_rows(tile):
    rows = tile * TOP_K + N_EXPERTS * (PIECE - 1)
    return -(-rows // LANES) * LANES


def _dispatch(expert, n_tok, tile):
    tm = MOE_TILE
    n_assign = n_tok * TOP_K
    per_tile = tile * TOP_K
    n_tiles = n_tok // tile
    q_max = _piece_rows(tile) // PIECE
    chunk = LANES
    experts = jnp.arange(N_EXPERTS, dtype=jnp.int32)
    e_flat = expert.reshape(n_assign)
    onehot = e_flat[:, None] == experts[None, :]
    oh = onehot.astype(BF16).reshape(n_assign // chunk, chunk, N_EXPERTS)
    tri = (jnp.arange(chunk)[:, None] >= jnp.arange(chunk)[None, :]).astype(BF16)
    within = jnp.einsum("ij,cjk->cik", tri, oh, preferred_element_type=F32)
    total = within[:, chunk - 1, :]
    before = jnp.cumsum(total, axis=0) - total
    running = (within + before[:, None, :]).reshape(n_tiles, per_tile, N_EXPERTS)
    onehot = onehot.reshape(n_tiles, per_tile, N_EXPERTS)

    run_end = running[:, per_tile - 1, :]
    run_start = jnp.concatenate([jnp.zeros((1, N_EXPERTS), F32), run_end[:-1]], axis=0)
    pieces = ((run_end - run_start).astype(jnp.int32) + PIECE - 1) // PIECE
    run_rows = pieces * PIECE
    rows_before = jnp.cumsum(run_rows, axis=0) - run_rows
    counts = rows_before[-1] + run_rows[-1]
    padded = (counts + tm - 1) // tm * tm
    pad_ends = jnp.cumsum(padded)
    pad_starts = pad_ends - padded

    piece_end = jnp.cumsum(pieces, axis=1)
    piece_first = piece_end - pieces
    local_start = (piece_first * PIECE).astype(F32)
    local = running - 1.0 - run_start[:, None, :] + local_start[:, None, :]
    lp = jnp.sum(jnp.where(onehot, local, 0.0), axis=2).astype(jnp.int32)
    lp = lp.reshape(n_tok, TOP_K)

    q = jnp.arange(q_max, dtype=jnp.int32)[None, :]
    e_of_q = jnp.sum((piece_end[:, None, :] <= q[:, :, None]).astype(jnp.int32), axis=2)
    pick = e_of_q[:, :, None] == experts[None, None, :]
    run_slot = pad_starts[None, :] + rows_before
    first_q = jnp.sum(jnp.where(pick, piece_first[:, None, :], 0), axis=2)
    slot_q = jnp.sum(jnp.where(pick, run_slot[:, None, :], 0), axis=2)
    piece_slot = jnp.where(e_of_q < N_EXPERTS, slot_q + (q - first_q) * PIECE, 0)
    n_pieces = piece_end[:, N_EXPERTS - 1]

    n_blocks = -(-(n_assign + N_EXPERTS * (n_tiles * (PIECE - 1) + tm - 1)) // tm)
    block_start = jnp.arange(n_blocks, dtype=jnp.int32) * tm
    block_expert = jnp.minimum(
        jnp.sum((pad_ends[None, :] <= block_start[:, None]).astype(jnp.int32), axis=1), N_EXPERTS - 1)
    n_active = (pad_ends[-1] // tm).astype(jnp.int32).reshape(1)
    return (block_expert, n_active, lp, piece_slot.reshape(n_tiles, 1, q_max).astype(jnp.int32),
            n_pieces.astype(jnp.int32), n_blocks * tm)


def kernel(x, c, positions, t5_table, ada_w, ada_b, norm_mix_g, norm_ffn_g, w_in, mla_q_norm_g, mla_w_uq, mla_kv_norm_g, mla_w_ukv, fox_forget_b, conv_w, diff_lambda, diff_subln_g, w_out, router_group_w, router_group_b, router_expert_w, router_expert_b, expert_w_gate, expert_w_up, expert_w_down, final_norm_g):
    bsz, seq, d = x.shape
    n = bsz * seq
    tq = min(ATT_Q, seq)
    tk = min(ATT_K, seq)
    tc = min(CMB_TILE, seq)

    mods = _ada_call(c, ada_w, ada_b).reshape(DEPTH, bsz, ADA_CHUNKS, D_MODEL)
    cos_t, sin_t = _rope_tables(positions)
    bias_tiles = _t5_tiles(t5_table, tq, tk)
    xf = x.reshape(n, d)

    for layer in range(DEPTH):
        mod = mods[layer]
        w, wf_t = _prep_in_proj(w_in[layer])
        wuq, wukv = _prep_mla(mla_w_uq[layer], mla_w_ukv[layer])
        fb = _place(fox_forget_b[layer][None, :], 16, 0).T.astype(F32)
        qg = _place(mla_q_norm_g[layer][None, :], _CQ_PAD, 0)
        kvg = mla_kv_norm_g[layer][None, :]
        cw = _place(conv_w[layer].T, 8, 0).T
        (qa, ka, va, qb, kb, vb, nf, oc, qd, kd, vd) = _proj_call(
            xf, mod, norm_mix_g[layer][None, :], w, wf_t, fb, wuq, wukv, qg, kvg, cos_t, sin_t, cw, bsz, seq)

        nf_t = nf.reshape(bsz, 8, seq // tk, tk).transpose(0, 2, 1, 3)
        lam_init = 0.8 - 0.6 * math.exp(-0.3 * layer)
        lam_p = diff_lambda[layer].astype(F32)
        lam = jnp.exp(jnp.sum(lam_p[0] * lam_p[1])) - jnp.exp(jnp.sum(lam_p[2] * lam_p[3])) + lam_init
        lam_row = jnp.full((1, LANES), lam, F32)
        sg = (jnp.concatenate([diff_subln_g[layer]] * 2) * (1.0 - lam_init))[None, :].astype(F32)
        oa, ob, od = _attn_call(
            [("A", (qa, ka, va)), ("B", (qb, kb, vb, nf_t)), ("D", (qd, kd, vd, bias_tiles, lam_row, sg))],
            bsz, seq)

        wr = jnp.concatenate([router_group_w[layer], router_expert_w[layer]], axis=1)
        wr_t = _place(wr, 80, 0).T
        wr_hi = wr_t.astype(BF16)
        wr_lo = (wr_t - wr_hi.astype(F32)).astype(BF16)
        rb = _place(jnp.concatenate([router_group_b[layer], router_expert_b[layer]])[None, :], 80, 0).T
        x1, h2, route = _out_router_call(
            oa, ob, oc, od, w_out[layer].astype(BF16), xf, mod, norm_ffn_g[layer][None, :],
            wr_hi, wr_lo, rb.astype(F32), bsz, seq)

        route_t = route.T
        expert = route_t[:, 2:4].astype(jnp.int32)
        block_expert, n_active, lp, piece_slot, n_pieces, n_slots = _dispatch(expert, n, tc)
        lp_rows = jnp.full((n // tc, 8, tc), -1, jnp.int32).at[:, 0:TOP_K, :].set(
            lp.reshape(n // tc, tc, TOP_K).transpose(0, 2, 1))
        x_slots = _dispatch_call(n_pieces, piece_slot, lp_rows, h2, n_slots)
        y_slots = _expert_call(block_expert, n_active, x_slots,
                               expert_w_gate, expert_w_up, expert_w_down, layer)
        route_lp = jnp.concatenate([route_t[:, 0:4], lp.astype(F32), route_t[:, 6:8]], axis=1)
        xf = _combine_call(piece_slot, y_slots, x1, route_lp, mod, final_norm_g[None, :], bsz, seq,
                           final=(layer == DEPTH - 1))
    return xf.reshape(bsz, seq, d)
```

```python
import functools
import math

import jax
import jax.numpy as jnp
from jax import lax
from jax.experimental import pallas as pl
from jax.experimental.pallas import tpu as pltpu

F32 = jnp.float32
BF16 = jnp.bfloat16

D_MODEL = 1024
DEPTH = 4
CHUNK = 64
EPS = 1e-6
HEAD_DIM = 64
GROUP_WIDTH = 256
MLA_NOPE = 64
MLA_ROPE = 32
MLA_Q_LORA = 192
MLA_KV_LORA = 128
ROPE_THETA = 10000.0
DIFF_HALF = 32
T5_BUCKETS = 32
T5_MAX_DIST = 128
N_GROUPS = 8
EXPERTS_PER_GROUP = 8
N_EXPERTS = 64
TOP_K = 2
D_EXPERT = 512
ADA_CHUNKS = 6

LANES = 128
LOG2E = 1.4426950408889634
NEG = -1e30

_O_FOX = MLA_Q_LORA + MLA_KV_LORA + MLA_ROPE
_O_CONV = _O_FOX + 3 * GROUP_WIDTH + 4
_O_DIFF = _O_CONV + 3 * GROUP_WIDTH

_CQ_PAD = 256
_W_A = _CQ_PAD + MLA_KV_LORA + 2 * LANES
_W_B = 3 * GROUP_WIDTH
_W_C = 3 * GROUP_WIDTH
_W_D = 3 * GROUP_WIDTH
_P_TOTAL = _W_A + _W_B + _W_C + _W_D

ROW_TILE = 512
ATT_Q = 256
ATT_K = 256
MOE_TILE = 1024
CMB_TILE = 512
PIECE = 8
VMEM_LIMIT = 56 * 1024 * 1024


def _cparams(sem):
    return pltpu.CompilerParams(dimension_semantics=sem, vmem_limit_bytes=VMEM_LIMIT)


def _split_bf16(a):
    hi = a.astype(BF16)
    lo = (a - hi.astype(F32)).astype(BF16)
    return hi, lo


def _dot(a, b):
    return jnp.dot(a, b, preferred_element_type=F32)


def _dot_nt(a, b):
    return lax.dot_general(a, b, (((1,), (1,)), ((), ())), preferred_element_type=F32)


def _sigmoid(z):
    return 1.0 / (1.0 + jnp.exp(-z))


_HIGH16 = -65536


def _pack_bf16_pair(lo, hi):
    lo_bits = pltpu.bitcast(lo.astype(BF16).astype(F32), jnp.int32)
    hi_bits = pltpu.bitcast(hi.astype(BF16).astype(F32), jnp.int32)
    sixteen = jnp.full(lo_bits.shape, 16, jnp.int32)
    return (hi_bits & _HIGH16) | lax.shift_right_logical(lo_bits, sixteen)


def _unpack_bf16_pair(packed):
    lo = pltpu.bitcast(lax.shift_left(packed, jnp.full(packed.shape, 16, jnp.int32)), F32)
    hi = pltpu.bitcast(packed & _HIGH16, F32)
    return lo.astype(BF16), hi.astype(BF16)


def _ada_kernel(c_ref, w_ref, b_ref, o_ref):
    c = c_ref[...]
    cond = c * _sigmoid(c)
    c_hi, c_lo = _split_bf16(cond)
    w_hi, w_lo = _split_bf16(w_ref[0])
    o_ref[0] = _dot(c_hi, w_hi) + _dot(c_hi, w_lo) + _dot(c_lo, w_hi) + b_ref[0]


def _ada_call(c, ada_w, ada_b):
    bsz = c.shape[0]
    n_col = ADA_CHUNKS * D_MODEL // D_MODEL
    return pl.pallas_call(
        _ada_kernel,
        grid=(DEPTH, n_col),
        in_specs=[
            pl.BlockSpec((bsz, D_MODEL), lambda l, j: (0, 0)),
            pl.BlockSpec((1, D_MODEL, D_MODEL), lambda l, j: (l, 0, j)),
            pl.BlockSpec((1, 1, D_MODEL), lambda l, j: (l, 0, j)),
        ],
        out_specs=pl.BlockSpec((1, bsz, D_MODEL), lambda l, j: (l, 0, j)),
        out_shape=jax.ShapeDtypeStruct((DEPTH, bsz, ADA_CHUNKS * D_MODEL), F32),
        compiler_params=_cparams(("arbitrary", "arbitrary")),
        name="ada_mod",
    )(c, ada_w, ada_b.reshape(DEPTH, 1, ADA_CHUNKS * D_MODEL))


def _values_with_ones(v):
    lane = lax.broadcasted_iota(jnp.int32, (1, LANES), 1)
    low_half = lane < HEAD_DIM
    slabs = []
    for head in range(4):
        pair = v[:, (head // 2) * LANES:(head // 2 + 1) * LANES]
        keep = low_half if head % 2 == 0 else jnp.logical_not(low_half)
        slabs.append(jnp.where(keep, pair, 1.0))
    return jnp.concatenate(slabs, axis=-1).astype(BF16)


def _proj_kernel(x_ref, mod_ref, g_ref, w_ref, wf_ref, fb_ref, wuq_ref, wukv_ref, qg_ref, kvg_ref,
                 cos_ref, sin_ref, cw_ref,
                 qa_ref, ka_ref, va_ref, qb_ref, kb_ref, vb_ref, nf_ref, oc_ref, qd_ref, kd_ref, vd_ref,
                 zc_ref, fc_ref, *, scale_a, scale_b, scale_d):
    t = pl.program_id(1)
    rows = x_ref.shape[0]

    @pl.when(t == 0)
    def _():
        zc_ref[...] = jnp.zeros_like(zc_ref)
        fc_ref[...] = jnp.zeros_like(fc_ref)

    x = x_ref[...]
    mod = mod_ref[0]
    hn = x * lax.rsqrt(jnp.mean(x * x, axis=-1, keepdims=True) + EPS) * g_ref[...]
    h = hn * (1.0 + mod[1:2]) + mod[0:1]
    hb = h.astype(BF16)

    z = _dot_nt(wf_ref[...], hb) + fb_ref[...]
    log_f = jnp.minimum(z, 0.0) - jnp.log(1.0 + jnp.exp(-jnp.abs(z)))
    r_i = lax.broadcasted_iota(jnp.int32, (rows, rows), 0)
    c_i = lax.broadcasted_iota(jnp.int32, (rows, rows), 1)
    tri = jnp.where(r_i <= c_i, 1.0, 0.0).astype(BF16)
    f_hi = log_f.astype(BF16)
    rem = log_f - f_hi.astype(F32)
    f_mid = rem.astype(BF16)
    f_lo = (rem - f_mid.astype(F32)).astype(BF16)
    cum = _dot(f_hi, tri) + _dot(f_mid, tri) + _dot(f_lo, tri) + fc_ref[:, 0:1]
    fc_ref[...] = jnp.broadcast_to(cum[:, rows - 1:rows], fc_ref.shape)
    nf_ref[0] = cum[0:8] * (-LOG2E)

    pa = _dot(hb, w_ref[:, 0:_W_A])
    cq = pa[:, 0:_CQ_PAD]
    ckv = pa[:, _CQ_PAD:_CQ_PAD + MLA_KV_LORA]
    kr = pa[:, _CQ_PAD + MLA_KV_LORA:_CQ_PAD + MLA_KV_LORA + LANES]
    krr = pa[:, _CQ_PAD + MLA_KV_LORA + LANES:_W_A]
    cqn = cq * lax.rsqrt(jnp.sum(cq * cq, axis=-1, keepdims=True) * (1.0 / MLA_Q_LORA) + EPS) * qg_ref[...]
    ckvn = ckv * lax.rsqrt(jnp.mean(ckv * ckv, axis=-1, keepdims=True) + EPS) * kvg_ref[...]
    q2 = _dot(cqn.astype(BF16), wuq_ref[...])
    kv2 = _dot(ckvn.astype(BF16), wukv_ref[...])
    cos = cos_ref[...]
    sin = sin_ref[...]
    cos4 = jnp.concatenate([cos] * 4, axis=-1)
    sin4 = jnp.concatenate([sin] * 4, axis=-1)
    qa = (q2[:, 0:512] * cos4 + q2[:, 512:1024] * sin4) * (scale_a * LOG2E)
    k_rope = kr * cos + krr * sin
    ka = kv2[:, 0:512] + jnp.concatenate([k_rope] * 4, axis=-1)
    qa_ref[...] = qa.astype(BF16)
    ka_ref[...] = ka.astype(BF16)
    va_ref[...] = _values_with_ones(kv2[:, 512:768])

    pb = _dot(hb, w_ref[:, _W_A:_W_A + _W_B])
    qb_ref[...] = (pb[:, 0:256] * (scale_b * LOG2E)).astype(BF16)
    kb_ref[...] = pb[:, 256:512].astype(BF16)
    vb_ref[...] = _values_with_ones(pb[:, 512:768])

    pc = _dot(hb, w_ref[:, _W_A + _W_B:_W_A + _W_B + _W_C])
    zz = pc[:, 256:512] * pc[:, 512:768]
    ext = jnp.concatenate([zc_ref[...], zz], axis=0)
    cw = cw_ref[...]
    conv = zz * cw[2:3] + ext[7:rows + 7] * cw[1:2] + ext[6:rows + 6] * cw[0:1]
    zc_ref[...] = zz[rows - 8:rows]
    oc_ref[...] = (pc[:, 0:256] * conv).astype(BF16)

    pd = _dot(hb, w_ref[:, _W_A + _W_B + _W_C:_P_TOTAL])
    qd_ref[...] = (pd[:, 0:256] * (scale_d * LOG2E)).astype(BF16)
    kd_ref[...] = pd[:, 256:512].astype(BF16)
    vd_ref[...] = _values_with_ones(pd[:, 512:768])


def _proj_call(x, mod, g, w, wf, fb, wuq, wukv, qg, kvg, cos_t, sin_t, cw, bsz, seq):
    n = bsz * seq
    rt = min(ROW_TILE, seq)
    nt = seq // rt
    row = lambda width: pl.BlockSpec((rt, width), lambda b, t: (b * nt + t, 0))
    full = lambda a: pl.BlockSpec(a.shape, lambda b, t: (0,) * a.ndim)
    bf = lambda width: jax.ShapeDtypeStruct((n, width), BF16)
    kern = functools.partial(
        _proj_kernel,
        scale_a=(MLA_NOPE + MLA_ROPE) ** -0.5, scale_b=HEAD_DIM ** -0.5, scale_d=DIFF_HALF ** -0.5)
    return pl.pallas_call(
        kern,
        grid=(bsz, nt),
        in_specs=[
            row(D_MODEL),
            pl.BlockSpec((1, ADA_CHUNKS, D_MODEL), lambda b, t: (b, 0, 0)),
            full(g), full(w), full(wf), full(fb), full(wuq), full(wukv), full(qg), full(kvg),
            row(LANES), row(LANES), full(cw),
        ],
        out_specs=[
            row(512), row(512), row(512), row(256), row(256), row(512),
            pl.BlockSpec((1, 8, rt), lambda b, t: (b, 0, t)),
            row(256), row(256), row(256), row(512),
        ],
        out_shape=[
            bf(512), bf(512), bf(512), bf(256), bf(256), bf(512),
            jax.ShapeDtypeStruct((bsz, 8, seq), F32),
            bf(256), bf(256), bf(256), bf(512),
        ],
        scratch_shapes=[pltpu.VMEM((8, GROUP_WIDTH), F32), pltpu.VMEM((16, LANES), F32)],
        compiler_params=_cparams(("arbitrary", "arbitrary")),
        name="norm_in_proj",
    )(x, mod, g, w, wf, fb, wuq, wukv, qg, kvg, cos_t, sin_t, cw)


def _tile_update(q, k_t, v_t, m_ref, acc_ref, bias=None, mask=None):
    s = _dot_nt(q, k_t)
    if bias is not None:
        s = s + bias
    if mask is not None:
        s = jnp.where(mask, s, NEG)
    m_prev = m_ref[...]
    m_new = jnp.maximum(m_prev, jnp.max(s, axis=-1, keepdims=True))
    alpha = jnp.exp2(m_prev - m_new)
    p = jnp.exp2(s - jnp.concatenate([m_new] * (s.shape[1] // LANES), axis=-1))
    acc_ref[...] = alpha * acc_ref[...] + _dot(p.astype(BF16), v_t)
    m_ref[...] = m_new


_ATTN_INPUTS = {"A": 3, "B": 4, "D": 6}
_ATTN_MAPS = {"A": 4, "B": 4, "D": 8}


def _attn_kernel(*refs, kinds, tq, tk):
    pos = 0
    ins, outs, q_scrs, m_refs, acc_refs = {}, {}, {}, {}, {}
    for kd in kinds:
        ins[kd] = refs[pos:pos + _ATTN_INPUTS[kd]]
        pos += _ATTN_INPUTS[kd]
    for kd in kinds:
        outs[kd] = refs[pos]
        pos += 1
    for kd in kinds:
        nm = _ATTN_MAPS[kd]
        q_scrs[kd] = refs[pos]
        m_refs[kd] = refs[pos + 1:pos + 1 + nm]
        acc_refs[kd] = refs[pos + 1 + nm:pos + 1 + 2 * nm]
        pos += 1 + 2 * nm
    i = pl.program_id(1)
    ratio = tk // tq
    jd = i // ratio
    par = i % ratio
    off = par * tq
    lane = lax.broadcasted_iota(jnp.int32, (1, LANES), 1)
    low_half = lane < HEAD_DIM
    r_i = lax.broadcasted_iota(jnp.int32, (tq, tk), 0) + off
    c_i = lax.broadcasted_iota(jnp.int32, (tq, tk), 1)
    shift = CHUNK.bit_length() - 1
    diag_masks = {
        "frame": c_i <= r_i,
        "chunk": jnp.right_shift(c_i, shift) <= jnp.right_shift(r_i, shift),
    }

    maps = []
    for kd in kinds:
        q_ref, q_scr = ins[kd][0], q_scrs[kd]
        for head in range(4):
            slab = head // 2
            if kd == "A":
                q_scr[head] = q_ref[:, head * LANES:(head + 1) * LANES]
                maps.append((kd, head, head, head))
            elif kd == "B":
                q = q_ref[:, slab * LANES:(slab + 1) * LANES]
                sel = low_half if head % 2 == 0 else jnp.logical_not(low_half)
                q_scr[head] = jnp.where(sel, q, jnp.zeros_like(q))
                maps.append((kd, head, slab, head))
            else:
                q = q_ref[:, slab * LANES:(slab + 1) * LANES]
                for mp in range(2):
                    lo = (2 * (head % 2) + mp) * DIFF_HALF
                    sel = (lane >= lo) & (lane < lo + DIFF_HALF)
                    q_scr[2 * head + mp] = jnp.where(sel, q, jnp.zeros_like(q))
                    maps.append((kd, 2 * head + mp, slab, head))
        for idx in range(_ATTN_MAPS[kd]):
            m_refs[kd][idx][...] = jnp.full(m_refs[kd][idx].shape, NEG, F32)
            acc_refs[kd][idx][...] = jnp.zeros(acc_refs[kd][idx].shape, F32)

    def step(j, mode):
        start = pl.multiple_of(j * tk, tk)
        for kd, idx, k_slab, head in maps:
            k_t = ins[kd][1][pl.ds(start, tk), k_slab * LANES:(k_slab + 1) * LANES]
            v_t = ins[kd][2][pl.ds(start, tk), head * LANES:(head + 1) * LANES]
            bias = None
            if kd == "B":
                bias = ins[kd][3][0, j][head:head + 1, :]
            elif kd == "D" and mode != "far":
                bias = ins[kd][3][head, par, 0 if mode == "diag" else 1]
            mask = diag_masks["frame" if kd == "B" else "chunk"] if mode == "diag" else None
            _tile_update(q_scrs[kd][idx], k_t, v_t, m_refs[kd][idx], acc_refs[kd][idx],
                         bias=bias, mask=mask)

    def far_pair(jj, carry):
        step(2 * jj, "far")
        step(2 * jj + 1, "far")
        return carry

    n_far = jnp.maximum(jd - 1, 0)
    lax.fori_loop(0, n_far // 2, far_pair, 0)

    @pl.when(n_far % 2 == 1)
    def _():
        step(n_far - 1, "far")

    @pl.when(jd >= 1)
    def _():
        step(jd - 1, "prev")
        step(jd, "diag")

    @pl.when(jd == 0)
    def _():
        step(jd, "diag")

    def normalized(kd, idx):
        acc = acc_refs[kd][idx][...]
        return acc / pltpu.roll(acc, HEAD_DIM, axis=1)

    for kd in kinds:
        for slab in range(2):
            heads = []
            for sub in range(2):
                head = 2 * slab + sub
                if kd == "D":
                    lam = ins[kd][4][...]
                    heads.append(normalized(kd, 2 * head) - lam * normalized(kd, 2 * head + 1))
                else:
                    heads.append(normalized(kd, head))
            o_slab = jnp.where(low_half, heads[0], heads[1])
            if kd == "D":
                sq = o_slab * o_slab
                s_lo = jnp.sum(jnp.where(low_half, sq, 0.0), axis=-1, keepdims=True)
                s_hi = jnp.sum(jnp.where(low_half, 0.0, sq), axis=-1, keepdims=True)
                ms = jnp.where(low_half, s_lo, s_hi) * (1.0 / HEAD_DIM)
                o_slab = o_slab * lax.rsqrt(ms + EPS) * ins[kd][5][...]
            outs[kd][:, slab * LANES:(slab + 1) * LANES] = o_slab.astype(BF16)


def _attn_call(groups, bsz, seq):
    n = bsz * seq
    tq = min(ATT_Q, seq)
    tk = min(ATT_K, seq)
    nq = seq // tq
    kinds = tuple(kd for kd, _ in groups)
    operands, in_specs, scratch = [], [], []
    for kd, (q, k, v, *extra) in groups:
        operands += [q, k, v, *extra]
        in_specs += [
            pl.BlockSpec((tq, q.shape[1]), lambda b, i: (b * nq + i, 0)),
            pl.BlockSpec((seq, k.shape[1]), lambda b, i: (b, 0)),
            pl.BlockSpec((seq, v.shape[1]), lambda b, i: (b, 0)),
        ]
        if kd == "B":
            in_specs.append(pl.BlockSpec((1, seq // tk, 8, tk), lambda b, i: (b, 0, 0, 0)))
        else:
            in_specs += [pl.BlockSpec(e.shape, lambda b, i, nd=e.ndim: (0,) * nd) for e in extra]
        scratch += [pltpu.VMEM((_ATTN_MAPS[kd], tq, LANES), BF16)]
        scratch += [pltpu.VMEM((tq, LANES), F32) for _ in range(2 * _ATTN_MAPS[kd])]
    out_spec = pl.BlockSpec((tq, GROUP_WIDTH), lambda b, i: (b * nq + i, 0))
    return pl.pallas_call(
        functools.partial(_attn_kernel, kinds=kinds, tq=tq, tk=tk),
        grid=(bsz, nq),
        in_specs=in_specs,
        out_specs=[out_spec] * len(kinds),
        out_shape=[jax.ShapeDtypeStruct((n, GROUP_WIDTH), BF16)] * len(kinds),
        scratch_shapes=scratch,
        compiler_params=_cparams(("arbitrary", "arbitrary")),
        name="attn_" + "".join(kinds),
    )(*operands)


def _out_router_kernel(oa_ref, ob_ref, oc_ref, od_ref, wo_ref, x_ref, mod_ref, g_ref, wr_hi_ref, wr_lo_ref,
                       rb_ref, xo_ref, h2_ref, route_ref):
    gw = GROUP_WIDTH
    mix = (_dot(oa_ref[...], wo_ref[0:gw]) + _dot(ob_ref[...], wo_ref[gw:2 * gw])
           + _dot(oc_ref[...], wo_ref[2 * gw:3 * gw]) + _dot(od_ref[...], wo_ref[3 * gw:4 * gw]))
    mod = mod_ref[0]
    x1 = x_ref[...] + mod[2:3] * mix
    xo_ref[...] = x1
    hn = x1 * lax.rsqrt(jnp.mean(x1 * x1, axis=-1, keepdims=True) + EPS) * g_ref[...]
    h2 = hn * (1.0 + mod[4:5]) + mod[3:4]
    h2_ref[...] = h2.astype(BF16)

    a_hi, a_lo = _split_bf16(h2)
    w_hi = wr_hi_ref[...]
    lg = _dot_nt(w_hi, a_hi) + _dot_nt(w_hi, a_lo) + _dot_nt(wr_lo_ref[...], a_hi) + rb_ref[...]
    gl = lg[0:N_GROUPS]
    rows = gl.shape[1]
    iota8 = lax.broadcasted_iota(jnp.int32, (N_GROUPS, rows), 0)
    gmax = jnp.max(gl, axis=0, keepdims=True)
    grp = jnp.min(jnp.where(gl == gmax, iota8, N_GROUPS), axis=0, keepdims=True)
    p_grp = 1.0 / jnp.sum(jnp.exp(gl - gmax), axis=0, keepdims=True)
    esel = jnp.zeros((EXPERTS_PER_GROUP, rows), F32)
    for gi in range(N_GROUPS):
        lo = N_GROUPS + gi * EXPERTS_PER_GROUP
        esel = jnp.where(grp == gi, lg[lo:lo + EXPERTS_PER_GROUP], esel)
    ee = jnp.exp(esel - jnp.max(esel, axis=0, keepdims=True))
    ps = ee / jnp.sum(ee, axis=0, keepdims=True)
    p1 = jnp.max(ps, axis=0, keepdims=True)
    i1 = jnp.min(jnp.where(ps == p1, iota8, EXPERTS_PER_GROUP), axis=0, keepdims=True)
    ps2 = jnp.where(iota8 == i1, -1.0, ps)
    p2 = jnp.max(ps2, axis=0, keepdims=True)
    i2 = jnp.min(jnp.where(ps2 == p2, iota8, EXPERTS_PER_GROUP), axis=0, keepdims=True)
    den = p1 + p2
    g1 = p_grp * p1 / den
    g2 = p_grp * p2 / den
    e1 = (grp * EXPERTS_PER_GROUP + i1).astype(F32)
    e2 = (grp * EXPERTS_PER_GROUP + i2).astype(F32)
    zero = jnp.zeros_like(g1)
    route_ref[...] = jnp.concatenate([g1, g2, e1, e2, zero, zero, zero, zero], axis=0)


def _out_router_call(oa, ob, oc, od, wo, x, mod, g, wr_hi, wr_lo, rb, bsz, seq):
    n = bsz * seq
    rt = min(ROW_TILE, seq)
    nt = seq // rt
    row = lambda width: pl.BlockSpec((rt, width), lambda t: (t, 0))
    full = lambda a: pl.BlockSpec(a.shape, lambda t: (0,) * a.ndim)
    return pl.pallas_call(
        _out_router_kernel,
        grid=(n // rt,),
        in_specs=[
            row(GROUP_WIDTH), row(GROUP_WIDTH), row(GROUP_WIDTH), row(GROUP_WIDTH), full(wo),
            row(D_MODEL),
            pl.BlockSpec((1, ADA_CHUNKS, D_MODEL), lambda t: (t // nt, 0, 0)),
            full(g), full(wr_hi), full(wr_lo), full(rb),
        ],
        out_specs=[row(D_MODEL), row(D_MODEL), pl.BlockSpec((8, rt), lambda t: (0, t))],
        out_shape=[
            jax.ShapeDtypeStruct((n, D_MODEL), F32),
            jax.ShapeDtypeStruct((n, D_MODEL), BF16),
            jax.ShapeDtypeStruct((8, n), F32),
        ],
        compiler_params=_cparams(("arbitrary",)),
        name="out_proj_router",
    )(oa, ob, oc, od, wo, x, mod, g, wr_hi, wr_lo, rb)


def _dispatch_kernel(np_ref, slot_ref, lp_ref, h_ref, init_hbm, xs_hbm, sorted_buf, sem):
    del init_hbm
    t = pl.program_id(0)
    cur = t % 2
    rows_l = sorted_buf.shape[1]
    lp = lp_ref[0]
    pos = lax.broadcasted_iota(jnp.int32, (rows_l, lp.shape[1]), 0)
    hit = pos == lp[0:1, :]
    for k in range(1, TOP_K):
        hit = hit | (pos == lp[k:k + 1, :])
    perm = jnp.where(hit, 1.0, 0.0).astype(BF16)
    ordered = _dot(perm, h_ref[...])
    sorted_buf[cur] = _pack_bf16_pair(ordered[:, 0:D_MODEL // 2], ordered[:, D_MODEL // 2:D_MODEL])

    def piece(q, buf):
        src = sorted_buf.at[buf, pl.ds(pl.multiple_of(q * PIECE, PIECE), PIECE)]
        slot = pl.multiple_of(slot_ref[0, 0, q], PIECE)
        return pltpu.make_async_copy(src, xs_hbm.at[pl.ds(slot, PIECE)], sem.at[buf])

    def drain(count, buf):
        def body(q, carry):
            piece(q, buf).wait()
            return carry
        lax.fori_loop(0, count, body, 0)

    @pl.when(t > 0)
    def _():
        drain(np_ref[jnp.maximum(t - 1, 0)], 1 - cur)

    def issue(q, carry):
        piece(q, cur).start()
        return carry
    lax.fori_loop(0, np_ref[t], issue, 0)

    @pl.when(t + 1 == pl.num_programs(0))
    def _():
        drain(np_ref[t], cur)


def _dispatch_call(n_pieces, piece_slot, lp_rows, h2, n_slots):
    n = h2.shape[0]
    n_tiles, _, q_max = piece_slot.shape
    td = n // n_tiles
    init = jnp.zeros((n_slots, D_MODEL // 2), jnp.int32)
    grid_spec = pltpu.PrefetchScalarGridSpec(
        num_scalar_prefetch=1,
        grid=(n_tiles,),
        in_specs=[
            pl.BlockSpec((1, 1, q_max), lambda t, npc: (t, 0, 0), memory_space=pltpu.SMEM),
            pl.BlockSpec((1, 8, td), lambda t, npc: (t, 0, 0)),
            pl.BlockSpec((td, D_MODEL), lambda t, npc: (t, 0)),
            pl.BlockSpec(memory_space=pl.ANY),
        ],
        out_specs=pl.BlockSpec(memory_space=pl.ANY),
        scratch_shapes=[pltpu.VMEM((2, q_max * PIECE, D_MODEL // 2), jnp.int32),
                        pltpu.SemaphoreType.DMA((2,))],
    )
    return pl.pallas_call(
        _dispatch_kernel,
        grid_spec=grid_spec,
        out_shape=jax.ShapeDtypeStruct((n_slots, D_MODEL // 2), jnp.int32),
        input_output_aliases={4: 0},
        compiler_params=_cparams(("arbitrary",)),
        name="moe_dispatch",
    )(n_pieces, piece_slot, lp_rows, h2, init)


def _expert_kernel(be_ref, nact_ref, x_ref, wg_ref, wu_ref, wd_ref, y_ref, wgb, wub, wdb):
    i = pl.program_id(0)

    @pl.when(i < nact_ref[0])
    def _():
        prev = be_ref[jnp.maximum(i - 1, 0)]

        @pl.when((i == 0) | (be_ref[i] != prev))
        def _():
            wgb[...] = wg_ref[0, 0].astype(BF16)
            wub[...] = wu_ref[0, 0].astype(BF16)
            wdb[...] = wd_ref[0, 0].astype(BF16)

        half = D_MODEL // 2
        x_lo, x_hi = _unpack_bf16_pair(x_ref[...])
        gate = _dot(x_lo, wgb[0:half]) + _dot(x_hi, wgb[half:D_MODEL])
        up = _dot(x_lo, wub[0:half]) + _dot(x_hi, wub[half:D_MODEL])
        hid = gate * _sigmoid(gate) * up
        y = _dot(hid.astype(BF16), wdb[...])
        y_ref[...] = _pack_bf16_pair(y[:, 0:half], y[:, half:D_MODEL])

    @pl.when(i >= nact_ref[0])
    def _():
        y_ref[...] = jnp.zeros_like(y_ref)


def _expert_call(block_expert, n_active, x_slots, w_gate, w_up, w_down, layer):
    n_blocks = block_expert.shape[0]
    tm = MOE_TILE
    w_idx = lambda i, be, na: (layer, be[i], 0, 0)
    grid_spec = pltpu.PrefetchScalarGridSpec(
        num_scalar_prefetch=2,
        grid=(n_blocks,),
        in_specs=[
            pl.BlockSpec((tm, D_MODEL // 2), lambda i, be, na: (jnp.minimum(i, na[0] - 1), 0)),
            pl.BlockSpec((1, 1, D_MODEL, D_EXPERT), w_idx),
            pl.BlockSpec((1, 1, D_MODEL, D_EXPERT), w_idx),
            pl.BlockSpec((1, 1, D_EXPERT, D_MODEL), w_idx),
        ],
        out_specs=pl.BlockSpec((tm, D_MODEL // 2), lambda i, be, na: (i, 0)),
        scratch_shapes=[
            pltpu.VMEM((D_MODEL, D_EXPERT), BF16),
            pltpu.VMEM((D_MODEL, D_EXPERT), BF16),
            pltpu.VMEM((D_EXPERT, D_MODEL), BF16),
        ],
    )
    return pl.pallas_call(
        _expert_kernel,
        grid_spec=grid_spec,
        out_shape=jax.ShapeDtypeStruct((n_blocks * tm, D_MODEL // 2), jnp.int32),
        compiler_params=_cparams(("arbitrary",)),
        name="expert_mlp",
    )(block_expert, n_active, x_slots, w_gate, w_up, w_down)


def _combine_kernel(slot_ref, next_slot_ref, y_hbm, x_ref, route_ref, mod_ref, fg_ref, o_ref, ybuf, sem,
                    *, final):
    t = pl.program_id(0)
    n_q = slot_ref.shape[2]
    rows_l = ybuf.shape[1]

    def fetch(slots, buf):
        def issue(q, carry):
            dst = ybuf.at[buf, pl.ds(pl.multiple_of(q * PIECE, PIECE), PIECE)]
            slot = pl.multiple_of(slots[0, 0, q], PIECE)
            pltpu.make_async_copy(y_hbm.at[pl.ds(slot, PIECE)], dst, sem.at[buf]).start()
            return carry
        lax.fori_loop(0, n_q, issue, 0)

    cur = t % 2

    @pl.when(t == 0)
    def _():
        fetch(slot_ref, cur)

    @pl.when(t + 1 < pl.num_programs(0))
    def _():
        fetch(next_slot_ref, 1 - cur)

    pltpu.make_async_copy(y_hbm.at[pl.ds(0, rows_l)], ybuf.at[cur], sem.at[cur]).wait()

    rt = route_ref[...]
    pos = lax.broadcasted_iota(jnp.int32, (rt.shape[0], rows_l), 1)
    weights = jnp.zeros(pos.shape, F32)
    for k in range(TOP_K):
        lp_k = rt[:, 4 + k:5 + k].astype(jnp.int32)
        weights = jnp.where(pos == lp_k, rt[:, k:k + 1], weights)
    y_lo, y_hi = _unpack_bf16_pair(ybuf[cur])
    wb = weights.astype(BF16)
    y = jnp.concatenate([_dot(wb, y_lo), _dot(wb, y_hi)], axis=-1)
    x2 = x_ref[...] + mod_ref[0][5:6] * y
    if final:
        x2 = x2 * lax.rsqrt(jnp.mean(x2 * x2, axis=-1, keepdims=True) + EPS) * fg_ref[...]
    o_ref[...] = x2


def _combine_call(piece_slot, y_slots, x, route_lp, mod, fg, bsz, seq, final):
    n = bsz * seq
    n_tiles, _, q_max = piece_slot.shape
    tc = n // n_tiles
    nt = seq // tc
    return pl.pallas_call(
        functools.partial(_combine_kernel, final=final),
        grid=(n_tiles,),
        in_specs=[
            pl.BlockSpec((1, 1, q_max), lambda t: (t, 0, 0), memory_space=pltpu.SMEM),
            pl.BlockSpec((1, 1, q_max), lambda t: (jnp.minimum(t + 1, n_tiles - 1), 0, 0),
                         memory_space=pltpu.SMEM),
            pl.BlockSpec(memory_space=pl.ANY),
            pl.BlockSpec((tc, D_MODEL), lambda t: (t, 0)),
            pl.BlockSpec((tc, 8), lambda t: (t, 0)),
            pl.BlockSpec((1, ADA_CHUNKS, D_MODEL), lambda t: (t // nt, 0, 0)),
            pl.BlockSpec((1, D_MODEL), lambda t: (0, 0)),
        ],
        out_specs=pl.BlockSpec((tc, D_MODEL), lambda t: (t, 0)),
        out_shape=jax.ShapeDtypeStruct((n, D_MODEL), F32),
        scratch_shapes=[pltpu.VMEM((2, q_max * PIECE, D_MODEL // 2), jnp.int32),
                        pltpu.SemaphoreType.DMA((2,))],
        compiler_params=_cparams(("arbitrary",)),
        name="moe_combine",
    )(piece_slot, piece_slot, y_slots, x, route_lp, mod, fg)


def _t5_bucket(rel):
    nb = T5_BUCKETS // 2
    max_exact = nb // 2
    bucket = jnp.where(rel > 0, nb, 0)
    n = jnp.abs(rel)
    large = max_exact + (jnp.log(jnp.maximum(n, 1).astype(F32) / max_exact)
                         / math.log(T5_MAX_DIST / max_exact) * (nb - max_exact)).astype(jnp.int32)
    large = jnp.minimum(large, nb - 1)
    return bucket + jnp.where(n < max_exact, n, large)


def _t5_tiles(t5_table, tq, tk):
    assert tk >= T5_MAX_DIST and tk % tq == 0
    par = jnp.arange(tk // tq, dtype=jnp.int32)[:, None, None, None]
    back = jnp.arange(2, dtype=jnp.int32)[None, :, None, None]
    r = jnp.arange(tq, dtype=jnp.int32)[None, None, :, None]
    c = jnp.arange(tk, dtype=jnp.int32)[None, None, None, :]
    rel = c - back * tk - (r + par * tq)
    far = t5_table[_t5_bucket(jnp.array(-4 * T5_MAX_DIST, jnp.int32))]
    bucket = _t5_bucket(rel)[..., None]
    b = jnp.zeros(rel.shape + (t5_table.shape[1],), F32)
    for k in range(T5_BUCKETS):
        b = jnp.where(bucket == k, t5_table[k], b)
    b = b - far
    return (b.transpose(4, 0, 1, 2, 3) * LOG2E).astype(F32)


def _place(cols, width, offset):
    z = jnp.zeros((cols.shape[0], width), cols.dtype)
    return z.at[:, offset:offset + cols.shape[1]].set(cols)


def _rot_half(cols):
    half = cols.shape[1] // 2
    return jnp.concatenate([-cols[:, half:], cols[:, :half]], axis=1)


def _prep_in_proj(w_in):
    cq = _place(w_in[:, 0:MLA_Q_LORA], _CQ_PAD, 0)
    ckv = w_in[:, MLA_Q_LORA:MLA_Q_LORA + MLA_KV_LORA]
    kr = w_in[:, MLA_Q_LORA + MLA_KV_LORA:_O_FOX]
    kr_p = _place(kr, LANES, MLA_NOPE)
    kr_rot = _place(_rot_half(kr), LANES, MLA_NOPE)
    fox = w_in[:, _O_FOX:_O_FOX + 3 * GROUP_WIDTH]
    wf = w_in[:, _O_FOX + 3 * GROUP_WIDTH:_O_CONV]
    conv = w_in[:, _O_CONV:_O_DIFF]
    diff = w_in[:, _O_DIFF:_O_DIFF + 3 * GROUP_WIDTH]
    w = jnp.concatenate([cq, ckv, kr_p, kr_rot, fox, conv, diff], axis=1).astype(BF16)
    wf_t = _place(wf, 16, 0).T.astype(BF16)
    return w, wf_t


def _prep_mla(w_uq, w_ukv):
    hd = MLA_NOPE + MLA_ROPE
    plain, rot, k_nope, v = [], [], [], []
    for h in range(4):
        wh = w_uq[:, h * hd:(h + 1) * hd]
        plain.append(_place(wh, LANES, 0))
        rot.append(_place(_rot_half(wh[:, MLA_NOPE:]), LANES, MLA_NOPE))
        kv = w_ukv[:, h * 2 * HEAD_DIM:(h + 1) * 2 * HEAD_DIM]
        k_nope.append(_place(kv[:, :MLA_NOPE], LANES, 0))
        v.append(kv[:, MLA_NOPE:])
    wuq = jnp.concatenate(plain + rot, axis=1)
    wuq = jnp.concatenate([wuq, jnp.zeros((_CQ_PAD - MLA_Q_LORA, wuq.shape[1]), wuq.dtype)], axis=0)
    wukv = jnp.concatenate(k_nope + v, axis=1)
    return wuq.astype(BF16), wukv.astype(BF16)


def _rope_tables(positions):
    half = MLA_ROPE // 2
    inv_freq = ROPE_THETA ** (-jnp.arange(half, dtype=F32) / half)
    ang = positions.astype(F32).reshape(-1)[:, None] * inv_freq
    cos, sin = jnp.cos(ang), jnp.sin(ang)
    n = ang.shape[0]
    cos_t = jnp.ones((n, LANES), F32).at[:, MLA_NOPE:MLA_NOPE + MLA_ROPE].set(jnp.concatenate([cos, cos], 1))
    sin_t = jnp.zeros((n, LANES), F32).at[:, MLA_NOPE:MLA_NOPE + MLA_ROPE].set(jnp.concatenate([sin, sin], 1))
    return cos_t, sin_t


def _piece_rows(tile):
    rows = tile * TOP_K + N_EXPERTS * (PIECE - 1)
    return -(-rows // LANES) * LANES


def _dispatch(expert, n_tok, tile):
    tm = MOE_TILE
    n_assign = n_tok * TOP_K
    per_tile = tile * TOP_K
    n_tiles = n_tok // tile
    q_max = _piece_rows(tile) // PIECE
    chunk = LANES
    experts = jnp.arange(N_EXPERTS, dtype=jnp.int32)
    e_flat = expert.reshape(n_assign)
    onehot = e_flat[:, None] == experts[None, :]
    oh = onehot.astype(BF16).reshape(n_assign // chunk, chunk, N_EXPERTS)
    tri = (jnp.arange(chunk)[:, None] >= jnp.arange(chunk)[None, :]).astype(BF16)
    within = jnp.einsum("ij,cjk->cik", tri, oh, preferred_element_type=F32)
    total = within[:, chunk - 1, :]
    before = jnp.cumsum(total, axis=0) - total
    running = (within + before[:, None, :]).reshape(n_tiles, per_tile, N_EXPERTS)
    onehot = onehot.reshape(n_tiles, per_tile, N_EXPERTS)

    run_end = running[:, per_tile - 1, :]
    run_start = jnp.concatenate([jnp.zeros((1, N_EXPERTS), F32), run_end[:-1]], axis=0)
    pieces = ((run_end - run_start).astype(jnp.int32) + PIECE - 1) // PIECE
    run_rows = pieces * PIECE
    rows_before = jnp.cumsum(run_rows, axis=0) - run_rows
    counts = rows_before[-1] + run_rows[-1]
    padded = (counts + tm - 1) // tm * tm
    pad_ends = jnp.cumsum(padded)
    pad_starts = pad_ends - padded

    piece_end = jnp.cumsum(pieces, axis=1)
    piece_first = piece_end - pieces
    local_start = (piece_first * PIECE).astype(F32)
    local = running - 1.0 - run_start[:, None, :] + local_start[:, None, :]
    lp = jnp.sum(jnp.where(onehot, local, 0.0), axis=2).astype(jnp.int32)
    lp = lp.reshape(n_tok, TOP_K)

    q = jnp.arange(q_max, dtype=jnp.int32)[None, :]
    e_of_q = jnp.sum((piece_end[:, None, :] <= q[:, :, None]).astype(jnp.int32), axis=2)
    pick = e_of_q[:, :, None] == experts[None, None, :]
    run_slot = pad_starts[None, :] + rows_before
    first_q = jnp.sum(jnp.where(pick, piece_first[:, None, :], 0), axis=2)
    slot_q = jnp.sum(jnp.where(pick, run_slot[:, None, :], 0), axis=2)
    piece_slot = jnp.where(e_of_q < N_EXPERTS, slot_q + (q - first_q) * PIECE, 0)
    n_pieces = piece_end[:, N_EXPERTS - 1]

    n_blocks = -(-(n_assign + N_EXPERTS * (n_tiles * (PIECE - 1) + tm - 1)) // tm)
    block_start = jnp.arange(n_blocks, dtype=jnp.int32) * tm
    block_expert = jnp.minimum(
        jnp.sum((pad_ends[None, :] <= block_start[:, None]).astype(jnp.int32), axis=1), N_EXPERTS - 1)
    n_active = (pad_ends[-1] // tm).astype(jnp.int32).reshape(1)
    return (block_expert, n_active, lp, piece_slot.reshape(n_tiles, 1, q_max).astype(jnp.int32),
            n_pieces.astype(jnp.int32), n_blocks * tm)


def kernel(x, c, positions, t5_table, ada_w, ada_b, norm_mix_g, norm_ffn_g, w_in, mla_q_norm_g, mla_w_uq, mla_kv_norm_g, mla_w_ukv, fox_forget_b, conv_w, diff_lambda, diff_subln_g, w_out, router_group_w, router_group_b, router_expert_w, router_expert_b, expert_w_gate, expert_w_up, expert_w_down, final_norm_g):
    bsz, seq, d = x.shape
    n = bsz * seq
    tq = min(ATT_Q, seq)
    tk = min(ATT_K, seq)
    tc = min(CMB_TILE, seq)

    mods = _ada_call(c, ada_w, ada_b).reshape(DEPTH, bsz, ADA_CHUNKS, D_MODEL)
    cos_t, sin_t = _rope_tables(positions)
    bias_tiles = _t5_tiles(t5_table, tq, tk)
    xf = x.reshape(n, d)

    for layer in range(DEPTH):
        mod = mods[layer]
        w, wf_t = _prep_in_proj(w_in[layer])
        wuq, wukv = _prep_mla(mla_w_uq[layer], mla_w_ukv[layer])
        fb = _place(fox_forget_b[layer][None, :], 16, 0).T.astype(F32)
        qg = _place(mla_q_norm_g[layer][None, :], _CQ_PAD, 0)
        kvg = mla_kv_norm_g[layer][None, :]
        cw = _place(conv_w[layer].T, 8, 0).T
        (qa, ka, va, qb, kb, vb, nf, oc, qd, kd, vd) = _proj_call(
            xf, mod, norm_mix_g[layer][None, :], w, wf_t, fb, wuq, wukv, qg, kvg, cos_t, sin_t, cw, bsz, seq)

        nf_t = nf.reshape(bsz, 8, seq // tk, tk).transpose(0, 2, 1, 3)
        lam_init = 0.8 - 0.6 * math.exp(-0.3 * layer)
        lam_p = diff_lambda[layer].astype(F32)
        lam = jnp.exp(jnp.sum(lam_p[0] * lam_p[1])) - jnp.exp(jnp.sum(lam_p[2] * lam_p[3])) + lam_init
        lam_row = jnp.full((1, LANES), lam, F32)
        sg = (jnp.concatenate([diff_subln_g[layer]] * 2) * (1.0 - lam_init))[None, :].astype(F32)
        oa, ob, od = _attn_call(
            [("A", (qa, ka, va)), ("B", (qb, kb, vb, nf_t)), ("D", (qd, kd, vd, bias_tiles, lam_row, sg))],
            bsz, seq)

        wr = jnp.concatenate([router_group_w[layer], router_expert_w[layer]], axis=1)
        wr_t = _place(wr, 80, 0).T
        wr_hi = wr_t.astype(BF16)
        wr_lo = (wr_t - wr_hi.astype(F32)).astype(BF16)
        rb = _place(jnp.concatenate([router_group_b[layer], router_expert_b[layer]])[None, :], 80, 0).T
        x1, h2, route = _out_router_call(
            oa, ob, oc, od, w_out[layer].astype(BF16), xf, mod, norm_ffn_g[layer][None, :],
            wr_hi, wr_lo, rb.astype(F32), bsz, seq)

        route_t = route.T
        expert = route_t[:, 2:4].astype(jnp.int32)
        block_expert, n_active, lp, piece_slot, n_pieces, n_slots = _dispatch(expert, n, tc)
        lp_rows = jnp.full((n // tc, 8, tc), -1, jnp.int32).at[:, 0:TOP_K, :].set(
            lp.reshape(n // tc, tc, TOP_K).transpose(0, 2, 1))
        x_slots = _dispatch_call(n_pieces, piece_slot, lp_rows, h2, n_slots)
        y_slots = _expert_call(block_expert, n_active, x_slots,
                               expert_w_gate, expert_w_up, expert_w_down, layer)
        route_lp = jnp.concatenate([route_t[:, 0:4], lp.astype(F32), route_t[:, 6:8]], axis=1)
        xf = _combine_call(piece_slot, y_slots, x1, route_lp, mod, final_norm_g[None, :], bsz, seq,
                           final=(layer == DEPTH - 1))
    return xf.reshape(bsz, seq, d)
```
